```python
import math
import jax, jax.numpy as jnp
from jax import lax
import numpy as np

D_MODEL = 1024
BATCH = 4
SEQ = 8192
DEPTH = 1

CHUNK = 64
Q_BLOCK = 128
HEAD_DIM = 64
RMS_EPS = 1e-6
ROPE_THETA = 10000.0
DIFF_HEADS = 4
DIFF_V_DIM = 2 * HEAD_DIM
DIFF_QK_WIDTH = DIFF_HEADS * 2 * HEAD_DIM
DIFF_V_WIDTH = DIFF_HEADS * DIFF_V_DIM
POOL_GROUPS = 4
POOL_WINDOWS = (2, 4, 8, 16)
POOL_WIDTH = D_MODEL // 4
POOL_GROUP_DIM = POOL_WIDTH // POOL_GROUPS
MEM_HEADS = 4
MEM_LEN = 256
MEM_WIDTH = MEM_HEADS * HEAD_DIM
N_BRANCHES = 3
Q_OFF = 0
K_OFF = Q_OFF + DIFF_QK_WIDTH
V_OFF = K_OFF + DIFF_QK_WIDTH
POOL_OFF = V_OFF + DIFF_V_WIDTH
MQ_OFF = POOL_OFF + POOL_WIDTH
GATE_OFF = MQ_OFF + MEM_WIDTH
IN_WIDTH = GATE_OFF + N_BRANCHES * D_MODEL
N_EXPERTS = 32
TOP_K = 4
D_EXPERT = D_MODEL
SWIGLU_LIMIT = 7.0
SWIGLU_ALPHA = 1.702
EXPERT_BLOCK = 128
NEG_INF = -1e30

kernel_name = 'hybrid_diffattn_pool_mem_moe_block'


def rms_norm(x, gain):
    xf = x.astype(jnp.float32)
    y = xf * lax.rsqrt(jnp.mean(xf * xf, axis=-1, keepdims=True) + RMS_EPS)
    return (y * gain.astype(jnp.float32)).astype(x.dtype)


def rope_tables(seq):
    inv_freq = 1.0 / (ROPE_THETA ** (jnp.arange(0, HEAD_DIM, 2, dtype=jnp.float32) / HEAD_DIM))
    ang = jnp.arange(seq, dtype=jnp.float32)[:, None] * inv_freq[None, :]
    return jnp.cos(ang), jnp.sin(ang)


def apply_rope(t, cos, sin):
    half = HEAD_DIM // 2
    tf = t.astype(jnp.float32)
    t1, t2 = tf[..., :half], tf[..., half:]
    return jnp.concatenate([t1 * cos - t2 * sin, t2 * cos + t1 * sin], axis=-1).astype(t.dtype)


def diff_attention(q, k, v, lam, sub_gain, lam_init):
    B, S = q.shape[0], q.shape[1]
    nqb = S // Q_BLOCK
    kt = k.transpose(0, 2, 3, 1, 4)
    vt = v.transpose(0, 2, 1, 3)
    qb = q.reshape(B, nqb, Q_BLOCK, 2, DIFF_HEADS, HEAD_DIM).transpose(1, 0, 3, 4, 2, 5)
    key_chunk = jnp.arange(S) // CHUNK
    scale = HEAD_DIM ** -0.5

    def block(args):
        q_blk, i = args
        s = jnp.einsum('bchqd,bchkd->bchqk', q_blk, kt).astype(jnp.float32) * scale
        q_chunk = (i * Q_BLOCK + jnp.arange(Q_BLOCK)) // CHUNK
        allowed = key_chunk[None, :] <= q_chunk[:, None]
        s = jnp.where(allowed, s, NEG_INF)
        p = jax.nn.softmax(s, axis=-1)
        a = p[:, 0] - lam * p[:, 1]
        return jnp.einsum('bhqk,bhkd->bhqd', a.astype(vt.dtype), vt)

    o = lax.map(block, (qb, jnp.arange(nqb)))
    o = o.transpose(1, 0, 3, 2, 4).reshape(B, S, DIFF_HEADS, DIFF_V_DIM)
    o = rms_norm(o, sub_gain) * (1.0 - lam_init)
    return o.reshape(B, S, DIFF_V_WIDTH)


def pool_mixer(u, w_group, scale):
    B, S, _ = u.shape
    uf = u.astype(jnp.float32).reshape(B, S, POOL_GROUPS, POOL_GROUP_DIM)
    c = jnp.cumsum(uf, axis=1)
    t1 = jnp.arange(1, S + 1, dtype=jnp.float32)
    outs = []
    for g, w in enumerate(POOL_WINDOWS):
        cg = c[:, :, g]
        lag = jnp.pad(cg, ((0, 0), (w, 0), (0, 0)))[:, :S]
        mean = (cg - lag) / jnp.minimum(t1, float(w))[None, :, None]
        outs.append(mean - uf[:, :, g])
    pooled = jnp.stack(outs, axis=2)
    mixed = jnp.einsum('bsgc,gcd->bsgd', pooled, w_group.astype(jnp.float32))
    return (mixed.reshape(B, S, POOL_WIDTH) * scale.astype(jnp.float32)).astype(u.dtype)


def memory_attention(mq, mem_n, w_mem_kv, mk_gain):
    B, S = mq.shape[0], mq.shape[1]
    M = mem_n.shape[1]
    kv = mem_n @ w_mem_kv
    mk = rms_norm(kv[..., :MEM_WIDTH].reshape(B, M, MEM_HEADS, HEAD_DIM), mk_gain)
    mv = kv[..., MEM_WIDTH:].reshape(B, M, MEM_HEADS, HEAD_DIM)
    s = jnp.einsum('bshd,bmhd->bhsm', mq, mk).astype(jnp.float32) * HEAD_DIM ** -0.5
    p = jax.nn.softmax(s, axis=-1)
    o = jnp.einsum('bhsm,bmhd->bshd', p.astype(mv.dtype), mv)
    return o.reshape(B, S, MEM_WIDTH)


def moe(h, w_router, b_router, w_gate_up, b_gate_up, w_down, b_down):
    Bt, S, D = h.shape
    T = Bt * S
    xf = h.reshape(T, D)
    logits = (xf @ w_router + b_router).astype(jnp.float32)
    top_val, top_idx = lax.top_k(logits, TOP_K)
    gate_w = jax.nn.softmax(top_val, axis=-1)
    A = T * TOP_K
    e_flat = top_idx.reshape(A).astype(jnp.int32)
    tok_flat = jnp.arange(A, dtype=jnp.int32) // TOP_K
    w_flat = gate_w.reshape(A)
    order = jnp.argsort(e_flat)
    e_sorted = e_flat[order]
    counts = jnp.bincount(e_flat, length=N_EXPERTS).astype(jnp.int32)
    padded = ((counts + EXPERT_BLOCK - 1) // EXPERT_BLOCK) * EXPERT_BLOCK
    pend = jnp.cumsum(padded)
    pstart = pend - padded
    ustart = jnp.cumsum(counts) - counts
    dest = pstart[e_sorted] + (jnp.arange(A, dtype=jnp.int32) - ustart[e_sorted])
    P = A + N_EXPERTS * EXPERT_BLOCK
    NB = P // EXPERT_BLOCK
    row_tok = jnp.full((P,), T, dtype=jnp.int32).at[dest].set(tok_flat[order])
    row_w = jnp.zeros((P,), jnp.float32).at[dest].set(w_flat[order])
    blk_exp = jnp.minimum(jnp.searchsorted(pend, jnp.arange(NB, dtype=jnp.int32) * EXPERT_BLOCK, side='right'),
                          N_EXPERTS - 1)
    xpad = jnp.concatenate([xf, jnp.zeros((1, D), xf.dtype)], axis=0)

    def expert_block(args):
        tok, e = args
        xb = xpad[tok]
        gu = (xb @ w_gate_up[e] + b_gate_up[e]).astype(jnp.float32)
        gate, up = gu[:, :D_EXPERT], gu[:, D_EXPERT:]
        gate = jnp.minimum(gate, SWIGLU_LIMIT)
        up = jnp.clip(up, -SWIGLU_LIMIT, SWIGLU_LIMIT)
        act = gate * jax.nn.sigmoid(SWIGLU_ALPHA * gate) * (up + 1.0)
        return act.astype(xb.dtype) @ w_down[e] + b_down[e]

    yb = lax.map(expert_block, (row_tok.reshape(NB, EXPERT_BLOCK), blk_exp))
    y = jnp.zeros((T + 1, D), jnp.float32).at[row_tok].add(yb.reshape(P, D).astype(jnp.float32) * row_w[:, None])
    return y[:T].reshape(Bt, S, D).astype(h.dtype)


def setup_inputs(seed: int = 0) -> dict:
    key = jax.random.key(seed)
    ks = jax.random.split(key, 32)
    L, D, E, F = DEPTH, D_MODEL, N_EXPERTS, D_EXPERT

    def nrm(k, shape, fan_in):
        return jax.random.normal(k, shape, jnp.float32) * (float(fan_in) ** -0.5)

    def gain(k, shape):
        return 1.0 + 0.02 * jax.random.normal(k, shape, jnp.float32)

    def small(k, shape, s):
        return s * jax.random.normal(k, shape, jnp.float32)

    return {
        'x': jax.random.normal(ks[0], (BATCH, SEQ, D), jnp.float32),
        'mem': jax.random.normal(ks[1], (BATCH, MEM_LEN, D), jnp.float32),
        'norm1': gain(ks[2], (L, D)),
        'w_in': nrm(ks[3], (L, D, IN_WIDTH), D),
        'b_gate': small(ks[4], (L, N_BRANCHES * D), 0.02),
        'q_norm': gain(ks[5], (L, HEAD_DIM)),
        'k_norm': gain(ks[6], (L, HEAD_DIM)),
        'lambda_q1': small(ks[7], (L, HEAD_DIM), 0.1),
        'lambda_k1': small(ks[8], (L, HEAD_DIM), 0.1),
        'lambda_q2': small(ks[9], (L, HEAD_DIM), 0.1),
        'lambda_k2': small(ks[10], (L, HEAD_DIM), 0.1),
        'diff_subln': gain(ks[11], (L, DIFF_V_DIM)),
        'w_pool': nrm(ks[12], (L, POOL_GROUPS, POOL_GROUP_DIM, POOL_GROUP_DIM), POOL_GROUP_DIM),
        'pool_scale': gain(ks[13], (L, POOL_WIDTH)),
        'mem_norm': gain(ks[14], (L, D)),
        'w_mem_kv': nrm(ks[15], (L, D, 2 * MEM_WIDTH), D),
        'mq_norm': gain(ks[16], (L, HEAD_DIM)),
        'mk_norm': gain(ks[17], (L, HEAD_DIM)),
        'w_br_diff': nrm(ks[18], (L, DIFF_V_WIDTH, D), DIFF_V_WIDTH),
        'w_br_pool': nrm(ks[19], (L, POOL_WIDTH, D), POOL_WIDTH),
        'w_br_mem': nrm(ks[20], (L, MEM_WIDTH, D), MEM_WIDTH),
        'w_out': nrm(ks[21], (L, D, D), D),
        'norm2': gain(ks[22], (L, D)),
        'w_router': nrm(ks[23], (L, D, E), D),
        'b_router': small(ks[24], (L, E), 0.01),
        'w_gate_up': nrm(ks[25], (L, E, D, 2 * F), D),
        'b_gate_up': small(ks[26], (L, E, 2 * F), 0.01),
        'w_down': nrm(ks[27], (L, E, F, D), F),
        'b_down': small(ks[28], (L, E, D), 0.01),
    }


def reference(x, mem, norm1, w_in, b_gate, q_norm, k_norm, lambda_q1, lambda_k1, lambda_q2, lambda_k2,
              diff_subln, w_pool, pool_scale, mem_norm, w_mem_kv, mq_norm, mk_norm, w_br_diff, w_br_pool,
              w_br_mem, w_out, norm2, w_router, b_router, w_gate_up, b_gate_up, w_down, b_down):
    B, S, D = x.shape
    cos, sin = rope_tables(S)
    cos_q, sin_q = cos[None, :, None, None, :], sin[None, :, None, None, :]
    for l in range(DEPTH):
        lam_init = 0.8 - 0.6 * math.exp(-0.3 * l)
        h = rms_norm(x, norm1[l])
        z = h @ w_in[l]
        q = z[..., Q_OFF:K_OFF].reshape(B, S, 2, DIFF_HEADS, HEAD_DIM)
        k = z[..., K_OFF:V_OFF].reshape(B, S, 2, DIFF_HEADS, HEAD_DIM)
        v = z[..., V_OFF:POOL_OFF].reshape(B, S, DIFF_HEADS, DIFF_V_DIM)
        q = apply_rope(rms_norm(q, q_norm[l]), cos_q, sin_q)
        k = apply_rope(rms_norm(k, k_norm[l]), cos_q, sin_q)
        lam = (jnp.exp(jnp.sum(lambda_q1[l].astype(jnp.float32) * lambda_k1[l].astype(jnp.float32)))
               - jnp.exp(jnp.sum(lambda_q2[l].astype(jnp.float32) * lambda_k2[l].astype(jnp.float32)))
               + lam_init)
        y_a = diff_attention(q, k, v, lam, diff_subln[l], lam_init)
        y_b = pool_mixer(z[..., POOL_OFF:MQ_OFF], w_pool[l], pool_scale[l])
        mq = rms_norm(z[..., MQ_OFF:GATE_OFF].reshape(B, S, MEM_HEADS, HEAD_DIM), mq_norm[l])
        y_c = memory_attention(mq, rms_norm(mem, mem_norm[l]), w_mem_kv[l], mk_norm[l])
        gates = jax.nn.sigmoid((z[..., GATE_OFF:] + b_gate[l]).astype(jnp.float32)).reshape(B, S, N_BRANCHES, D)
        merged = (gates[:, :, 0] * (y_a @ w_br_diff[l]).astype(jnp.float32)
                  + gates[:, :, 1] * (y_b @ w_br_pool[l]).astype(jnp.float32)
                  + gates[:, :, 2] * (y_c @ w_br_mem[l]).astype(jnp.float32))
        x = x + merged.astype(x.dtype) @ w_out[l]
        h2 = rms_norm(x, norm2[l])
        x = x + moe(h2, w_router[l], b_router[l], w_gate_up[l], b_gate_up[l], w_down[l], b_down[l])
    return x
```

```python
import functools
import math

import jax
import jax.numpy as jnp
from jax import lax
from jax.experimental import pallas as pl
from jax.experimental.pallas import tpu as pltpu

HEAD_DIM = 64
CHUNK = 64
RMS_EPS = 1e-6
ROPE_THETA = 10000.0
DIFF_HEADS = 4
MEM_HEADS = 4
POOL_WINDOWS = (2, 4, 8, 16)
POOL_HALO = 16
N_EXPERTS = 32
TOP_K = 4
SWIGLU_LIMIT = 7.0
SWIGLU_ALPHA = 1.702
NEG_INF = -1e30
LAM_INIT = 0.8 - 0.6 * math.exp(-0.3 * 0)

TOKEN_TILE = 512
EXPERT_ROWS = 256
VMEM_LIMIT = 56 * 1024 * 1024

_NT = (((1,), (1,)), ((), ()))


def _dot(a, b):
    return jnp.dot(a, b, preferred_element_type=jnp.float32)


def _dot_nt(a, b):
    return lax.dot_general(a, b, _NT, preferred_element_type=jnp.float32)


def _rms_rows(x, gain_row):
    return x * lax.rsqrt(jnp.mean(x * x, axis=-1, keepdims=True) + RMS_EPS) * gain_row


def _rms_cols(x, gain_col):
    return x * lax.rsqrt(jnp.mean(x * x, axis=0, keepdims=True) + RMS_EPS) * gain_col


def _mem_kv_kernel(mem_ref, gain_ref, wkv_t_ref, mk_gain_ref, mk_ref, mv_t_ref):
    mem_n = _rms_rows(mem_ref[0], gain_ref[...]).astype(jnp.bfloat16)
    kv_t = _dot_nt(wkv_t_ref[...], mem_n)
    mw = kv_t.shape[0] // 2
    for h in range(MEM_HEADS):
        blk = kv_t[h * HEAD_DIM:(h + 1) * HEAD_DIM]
        mk_t = _rms_cols(blk, mk_gain_ref[...]) * (HEAD_DIM ** -0.5)
        mk_ref[0, h] = mk_t.T.astype(jnp.bfloat16)
    mv_t_ref[0] = kv_t[mw:].astype(jnp.bfloat16)


def _mem_kv(mem, mem_norm, wkv_t, mk_gain_col):
    B, M, D = mem.shape
    mw = wkv_t.shape[0] // 2
    return pl.pallas_call(
        _mem_kv_kernel,
        grid=(B,),
        in_specs=[
            pl.BlockSpec((1, M, D), lambda b: (b, 0, 0)),
            pl.BlockSpec((1, D), lambda b: (0, 0)),
            pl.BlockSpec((2 * mw, D), lambda b: (0, 0)),
            pl.BlockSpec((HEAD_DIM, 1), lambda b: (0, 0)),
        ],
        out_specs=[
            pl.BlockSpec((1, MEM_HEADS, M, HEAD_DIM), lambda b: (b, 0, 0, 0)),
            pl.BlockSpec((1, mw, M), lambda b: (b, 0, 0)),
        ],
        out_shape=[
            jax.ShapeDtypeStruct((B, MEM_HEADS, M, HEAD_DIM), jnp.bfloat16),
            jax.ShapeDtypeStruct((B, mw, M), jnp.bfloat16),
        ],
        name="mem_kv",
    )(mem, mem_norm, wkv_t, mk_gain_col)


def _in_proj_kernel(x_ref, norm1_ref, wt_ref, wr_ref, gmat_ref, cos_t_ref, sin_t_ref, cos_k_ref, sin_k_ref,
                    qg_ref, kg_ref, kgs_ref, mqg_ref, wpool_ref, pscale_ref, mk_ref, mv_t_ref,
                    q_t_ref, k_ref, v_t_ref, yb_ref, yc_ref, halo_ref, ext_ref):
    s_idx = pl.program_id(1)
    tm = x_ref.shape[1]
    h = _rms_rows(x_ref[0], norm1_ref[...]).astype(jnp.bfloat16)
    z_t = _dot_nt(wt_ref[...], h)
    z_r = _dot(h, wr_ref[...])

    cos_t, sin_t = cos_t_ref[...], sin_t_ref[...]
    half = HEAD_DIM // 2
    for g in range(2 * DIFF_HEADS):
        y = _rms_cols(z_t[g * HEAD_DIM:(g + 1) * HEAD_DIM], qg_ref[...]) * (HEAD_DIM ** -0.5)
        t1, t2 = y[:half], y[half:]
        q_t_ref[0, g * HEAD_DIM:g * HEAD_DIM + half] = (t1 * cos_t - t2 * sin_t).astype(jnp.bfloat16)
        q_t_ref[0, g * HEAD_DIM + half:(g + 1) * HEAD_DIM] = (t2 * cos_t + t1 * sin_t).astype(jnp.bfloat16)

    qw = 2 * DIFF_HEADS * HEAD_DIM
    vd = 2 * HEAD_DIM
    for hd in range(DIFF_HEADS):
        v_t_ref[0, hd, 0] = z_t[qw + hd * vd:qw + (hd + 1) * vd].astype(jnp.bfloat16)

    mq_off = qw + DIFF_HEADS * vd
    outs = []
    for hd in range(MEM_HEADS):
        mq_t = _rms_cols(z_t[mq_off + hd * HEAD_DIM:mq_off + (hd + 1) * HEAD_DIM], mqg_ref[...])
        s_t = _dot(mk_ref[0, hd], mq_t.astype(jnp.bfloat16))
        p_t = jnp.exp(s_t - jnp.max(s_t, axis=0, keepdims=True))
        o_t = _dot(mv_t_ref[0, hd * HEAD_DIM:(hd + 1) * HEAD_DIM], p_t.astype(jnp.bfloat16))
        outs.append(o_t / jnp.sum(p_t, axis=0, keepdims=True))
    yc_ref[0] = jnp.concatenate(outs, axis=0).T.astype(jnp.bfloat16)

    kw = 2 * DIFF_HEADS * HEAD_DIM
    zk, zks = z_r[:, :kw], z_r[:, kw:2 * kw]
    ssq = _dot((zk * zk).astype(jnp.bfloat16), gmat_ref[...])
    r = lax.rsqrt(ssq * (1.0 / HEAD_DIM) + RMS_EPS)
    cos_k, sin_k = cos_k_ref[...], sin_k_ref[...]
    for j in range(kw // 128):
        sl = slice(j * 128, (j + 1) * 128)
        kk = r[:, sl] * (zk[:, sl] * kg_ref[...] * cos_k + zks[:, sl] * kgs_ref[...] * sin_k)
        k_ref[0, :, sl] = kk.astype(jnp.bfloat16)

    u = z_r[:, 2 * kw:]

    @pl.when(s_idx == 0)
    def _():
        halo_ref[...] = jnp.zeros_like(halo_ref)

    ext_ref[0:POOL_HALO] = halo_ref[...]
    ext_ref[POOL_HALO:] = u
    halo_ref[...] = u[tm - POOL_HALO:]
    pos1 = (s_idx * tm + 1 + lax.broadcasted_iota(jnp.int32, (tm, 128), 0)).astype(jnp.float32)
    lane = lax.broadcasted_iota(jnp.int32, (tm, 128), 1)
    pooled = []
    for part in range(2):
        cols = slice(part * 128, (part + 1) * 128)
        w_a, w_b = POOL_WINDOWS[2 * part], POOL_WINDOWS[2 * part + 1]
        acc = ext_ref[POOL_HALO:POOL_HALO + tm, cols]
        sum_a = None
        for j in range(1, w_b):
            if j == w_a:
                sum_a = acc
            acc = acc + ext_ref[POOL_HALO - j:POOL_HALO - j + tm, cols]
        mean_a = sum_a / jnp.minimum(pos1, float(w_a))
        mean_b = acc / jnp.minimum(pos1, float(w_b))
        pooled.append(jnp.where(lane < 64, mean_a, mean_b) - u[:, cols])
    pooled = jnp.concatenate(pooled, axis=1).astype(jnp.bfloat16)
    yb_ref[0] = (_dot(pooled, wpool_ref[...]) * pscale_ref[...]).astype(jnp.bfloat16)


def _in_proj(x, norm1, wt, wr, gmat, cos_t, sin_t, cos_k, sin_k, qg, kg, kgs, mqg, wpool, pscale, mk, mv_t):
    B, S, D = x.shape
    tm = TOKEN_TILE
    ns = S // tm
    nt, nr = wt.shape[0], wr.shape[1]
    qw = 2 * DIFF_HEADS * HEAD_DIM
    vd = 2 * HEAD_DIM
    pw = wpool.shape[0]
    M = mk.shape[2]
    mw = mv_t.shape[1]
    const = lambda b, s: (0, 0)
    return pl.pallas_call(
        _in_proj_kernel,
        grid=(B, ns),
        in_specs=[
            pl.BlockSpec((1, tm, D), lambda b, s: (b, s, 0)),
            pl.BlockSpec((1, D), const),
            pl.BlockSpec((nt, D), const),
            pl.BlockSpec((D, nr), const),
            pl.BlockSpec((qw, qw), const),
            pl.BlockSpec((HEAD_DIM // 2, tm), lambda b, s: (0, s)),
            pl.BlockSpec((HEAD_DIM // 2, tm), lambda b, s: (0, s)),
            pl.BlockSpec((tm, 128), lambda b, s: (s, 0)),
            pl.BlockSpec((tm, 128), lambda b, s: (s, 0)),
            pl.BlockSpec((HEAD_DIM, 1), const),
            pl.BlockSpec((1, 128), const),
            pl.BlockSpec((1, 128), const),
            pl.BlockSpec((HEAD_DIM, 1), const),
            pl.BlockSpec((pw, pw), const),
            pl.BlockSpec((1, pw), const),
            pl.BlockSpec((1, MEM_HEADS, M, HEAD_DIM), lambda b, s: (b, 0, 0, 0)),
            pl.BlockSpec((1, mw, M), lambda b, s: (b, 0, 0)),
        ],
        out_specs=[
            pl.BlockSpec((1, qw, tm), lambda b, s: (b, 0, s)),
            pl.BlockSpec((1, tm, qw), lambda b, s: (b, s, 0)),
            pl.BlockSpec((1, DIFF_HEADS, 1, vd, tm), lambda b, s: (b, 0, s, 0, 0)),
            pl.BlockSpec((1, tm, pw), lambda b, s: (b, s, 0)),
            pl.BlockSpec((1, tm, mw), lambda b, s: (b, s, 0)),
        ],
        out_shape=[
            jax.ShapeDtypeStruct((B, qw, S), jnp.bfloat16),
            jax.ShapeDtypeStruct((B, S, qw), jnp.bfloat16),
            jax.ShapeDtypeStruct((B, DIFF_HEADS, ns, vd, tm), jnp.bfloat16),
            jax.ShapeDtypeStruct((B, S, pw), jnp.bfloat16),
            jax.ShapeDtypeStruct((B, S, mw), jnp.bfloat16),
        ],
        scratch_shapes=[
            pltpu.VMEM((POOL_HALO, pw), jnp.float32),
            pltpu.VMEM((POOL_HALO + tm, pw), jnp.float32),
        ],
        compiler_params=pltpu.CompilerParams(
            dimension_semantics=("arbitrary", "arbitrary"), vmem_limit_bytes=VMEM_LIMIT),
        name="in_proj",
    )(x, norm1, wt, wr, gmat, cos_t, sin_t, cos_k, sin_k, qg, kg, kgs, mqg, wpool, pscale, mk, mv_t)


def _diff_attn_kernel(q_t_ref, k_ref, v_t_ref, lq1_ref, lk1_ref, lq2_ref, lk2_ref, subg_ref, o_ref,
                      rhs_ref, m_ref, l_ref, acc_ref):
    qi = pl.program_id(2)
    tq = q_t_ref.shape[2]
    tk = tq
    q_t = q_t_ref[0]
    row = lax.broadcasted_iota(jnp.int32, q_t.shape, 0)
    zero = jnp.zeros_like(q_t)
    rhs_ref[0] = jnp.where(row < HEAD_DIM, q_t, zero)
    rhs_ref[1] = jnp.where(row >= HEAD_DIM, q_t, zero)
    m_ref[...] = jnp.full(m_ref.shape, NEG_INF, jnp.float32)
    l_ref[...] = jnp.zeros(l_ref.shape, jnp.float32)
    acc_ref[...] = jnp.zeros(acc_ref.shape, jnp.float32)

    def step(j, masked):
        k_blk = k_ref[0, pl.ds(pl.multiple_of(j * tk, tk), tk), :]
        v_blk = v_t_ref[0, 0, j]
        for c in range(2):
            s = _dot(k_blk, rhs_ref[c])
            if masked:
                kc = lax.broadcasted_iota(jnp.int32, s.shape, 0) // CHUNK
                qc = lax.broadcasted_iota(jnp.int32, s.shape, 1) // CHUNK
                s = jnp.where(kc <= qc, s, NEG_INF)
            m_prev = m_ref[c]
            m_new = jnp.maximum(m_prev, jnp.max(s, axis=0, keepdims=True))
            alpha = jnp.exp(m_prev - m_new)
            p = jnp.exp(s - m_new)
            l_ref[c] = alpha * l_ref[c] + jnp.sum(p, axis=0, keepdims=True)
            acc_ref[c] = alpha * acc_ref[c] + _dot(v_blk, p.astype(jnp.bfloat16))
            m_ref[c] = m_new

    def body(j, carry):
        step(j, False)
        return carry

    lax.fori_loop(0, qi, body, 0)
    step(qi, True)

    lam = (jnp.exp(jnp.sum(lq1_ref[...] * lk1_ref[...], axis=-1, keepdims=True))
           - jnp.exp(jnp.sum(lq2_ref[...] * lk2_ref[...], axis=-1, keepdims=True)) + LAM_INIT)
    o = acc_ref[0] / l_ref[0] - lam * (acc_ref[1] / l_ref[1])
    y = _rms_cols(o, subg_ref[...]) * (1.0 - LAM_INIT)
    o_ref[0] = y.T.astype(jnp.bfloat16)


def _diff_attn(q_t, k, v_t, lq1, lk1, lq2, lk2, subg_col):
    B, qw, S = q_t.shape
    tq = TOKEN_TILE
    nq = S // tq
    vd = 2 * HEAD_DIM
    lam_spec = pl.BlockSpec((1, HEAD_DIM), lambda b, h, i: (0, 0))
    return pl.pallas_call(
        _diff_attn_kernel,
        grid=(B, DIFF_HEADS, nq),
        in_specs=[
            pl.BlockSpec((1, vd, tq), lambda b, h, i: (b, h, i)),
            pl.BlockSpec((1, S, vd), lambda b, h, i: (b, 0, h)),
            pl.BlockSpec((1, 1, nq, vd, tq), lambda b, h, i: (b, h, 0, 0, 0)),
            lam_spec, lam_spec, lam_spec, lam_spec,
            pl.BlockSpec((vd, 1), lambda b, h, i: (0, 0)),
        ],
        out_specs=pl.BlockSpec((1, tq, vd), lambda b, h, i: (b, i, h)),
        out_shape=jax.ShapeDtypeStruct((B, S, DIFF_HEADS * vd), jnp.bfloat16),
        scratch_shapes=[
            pltpu.VMEM((2, vd, tq), jnp.bfloat16),
            pltpu.VMEM((2, 1, tq), jnp.float32),
            pltpu.VMEM((2, 1, tq), jnp.float32),
            pltpu.VMEM((2, vd, tq), jnp.float32),
        ],
        compiler_params=pltpu.CompilerParams(
            dimension_semantics=("arbitrary", "arbitrary", "arbitrary"), vmem_limit_bytes=VMEM_LIMIT),
        name="diff_attn",
    )(q_t, k, v_t, lq1, lk1, lq2, lk2, subg_col)


def _merge_kernel(x_ref, norm1_ref, wg_ref, bg_ref, ya_ref, yb_ref, yc_ref, wa_ref, wb_ref, wc_ref, wo_ref,
                  norm2_ref, wr_hi_ref, wr_lo_ref, br_ref, x1_ref, h2_ref, idx_ref, gw_ref):
    D = x_ref.shape[2]
    x = x_ref[0]
    h = _rms_rows(x, norm1_ref[...]).astype(jnp.bfloat16)
    merged = None
    for i, (y_ref, w_ref) in enumerate(((ya_ref, wa_ref), (yb_ref, wb_ref), (yc_ref, wc_ref))):
        gz = _dot(h, wg_ref[:, i * D:(i + 1) * D]) + bg_ref[:, i * D:(i + 1) * D]
        gate = 1.0 / (1.0 + jnp.exp(-gz))
        term = gate * _dot(y_ref[0], w_ref[...])
        merged = term if merged is None else merged + term
    x1 = x + _dot(merged.astype(jnp.bfloat16), wo_ref[...])
    x1_ref[0] = x1
    h2 = _rms_rows(x1, norm2_ref[...])
    h2_ref[0] = h2

    hi = h2.astype(jnp.bfloat16)
    lo = (h2 - hi.astype(jnp.float32)).astype(jnp.bfloat16)
    logits = (_dot_nt(wr_hi_ref[...], hi) + _dot_nt(wr_hi_ref[...], lo) + _dot_nt(wr_lo_ref[...], hi)
              + br_ref[...])
    eidx = lax.broadcasted_iota(jnp.int32, logits.shape, 0)
    vals = logits
    top_v, top_i = [], []
    for _ in range(TOP_K):
        mx = jnp.max(vals, axis=0, keepdims=True)
        ix = jnp.min(jnp.where(vals == mx, eidx, N_EXPERTS), axis=0, keepdims=True)
        top_v.append(mx)
        top_i.append(ix)
        vals = jnp.where(eidx == ix, -jnp.inf, vals)
    ex = [jnp.exp(v - top_v[0]) for v in top_v]
    den = ex[0] + ex[1] + ex[2] + ex[3]
    idx_ref[0] = jnp.concatenate(top_i, axis=0)
    gw_ref[0] = jnp.concatenate([e / den for e in ex], axis=0)


def _merge(x, norm1, wg, bg, ya, yb, yc, wa, wb, wc, wo, norm2, wr_hi, wr_lo, br_col):
    B, S, D = x.shape
    tm = TOKEN_TILE
    const = lambda b, s: (0, 0)
    tok = lambda b, s: (b, s, 0)
    full = lambda a: pl.BlockSpec(a.shape, const)
    return pl.pallas_call(
        _merge_kernel,
        grid=(B, S // tm),
        in_specs=[
            pl.BlockSpec((1, tm, D), tok), full(norm1), full(wg), full(bg),
            pl.BlockSpec((1, tm, ya.shape[2]), tok), pl.BlockSpec((1, tm, yb.shape[2]), tok),
            pl.BlockSpec((1, tm, yc.shape[2]), tok),
            full(wa), full(wb), full(wc), full(wo), full(norm2), full(wr_hi), full(wr_lo), full(br_col),
        ],
        out_specs=[
            pl.BlockSpec((1, tm, D), tok), pl.BlockSpec((1, tm, D), tok),
            pl.BlockSpec((1, TOP_K, tm), lambda b, s: (b, 0, s)),
            pl.BlockSpec((1, TOP_K, tm), lambda b, s: (b, 0, s)),
        ],
        out_shape=[
            jax.ShapeDtypeStruct((B, S, D), jnp.float32),
            jax.ShapeDtypeStruct((B, S, D), jnp.float32),
            jax.ShapeDtypeStruct((B, TOP_K, S), jnp.int32),
            jax.ShapeDtypeStruct((B, TOP_K, S), jnp.float32),
        ],
        compiler_params=pltpu.CompilerParams(
            dimension_semantics=("arbitrary", "arbitrary"), vmem_limit_bytes=VMEM_LIMIT),
        name="merge",
    )(x, norm1, wg, bg, ya, yb, yc, wa, wb, wc, wo, norm2, wr_hi, wr_lo, br_col)


def _experts_kernel(nused_ref, bexp_ref, tok_cur_ref, tok_nxt_ref, dst_ref, h2_hbm, roww_ref, wgu_ref, bgu_ref,
                    wd_ref, bd_ref, y4_hbm, xbuf, ybuf, wgu_bf, wd_bf, gsem, ssem):
    b = pl.program_id(0)
    nb = pl.num_programs(0)
    nused = nused_ref[0]
    rows = xbuf.shape[1]
    F = wd_ref.shape[1]
    slot = b % 2

    def gather(tok_ref, buf_slot):
        def issue(i, carry):
            t = tok_ref[0, 0, i]
            pltpu.make_async_copy(h2_hbm.at[pl.ds(t, 1)], xbuf.at[buf_slot, pl.ds(i, 1)], gsem.at[buf_slot]).start()
            return carry
        lax.fori_loop(0, rows, issue, 0, unroll=8)

    def wait_rows(buf, sem):
        pltpu.make_async_copy(buf, buf, sem).wait()

    @pl.when(b == 0)
    def _():
        gather(tok_cur_ref, 0)
        ybuf[...] = jnp.zeros_like(ybuf)
        n_real = y4_hbm.shape[0] - 2 * rows
        for s in range(2):
            spare = pltpu.make_async_copy(ybuf.at[s], y4_hbm.at[pl.ds(n_real + s * rows, rows)], ssem.at[s])
            spare.start()
            spare.wait()

    @pl.when(b + 1 < nused)
    def _():
        gather(tok_nxt_ref, 1 - slot)

    @pl.when(jnp.logical_and(b >= 2, b - 2 < nused))
    def _():
        wait_rows(ybuf.at[slot], ssem.at[slot])

    changed = jnp.logical_or(b == 0, bexp_ref[b] != bexp_ref[jnp.maximum(b - 1, 0)])

    @pl.when(jnp.logical_and(changed, b < nused))
    def _():
        chunk = 128
        def cast(i, carry):
            r = pl.ds(pl.multiple_of(i * chunk, chunk), chunk)
            wgu_bf[r, :] = wgu_ref[0, r, :].astype(jnp.bfloat16)
            wd_bf[r, :] = wd_ref[0, r, :].astype(jnp.bfloat16)
            return carry
        lax.fori_loop(0, wgu_bf.shape[0] // chunk, cast, 0)

    @pl.when(jnp.logical_or(b == 0, b < nused))
    def _():
        wait_rows(xbuf.at[slot], gsem.at[slot])

    @pl.when(b < nused)
    def _():
        x = xbuf[slot].astype(jnp.bfloat16)
        gu = _dot(x, wgu_bf[...]) + bgu_ref[0]
        gate = jnp.minimum(gu[:, :F], SWIGLU_LIMIT)
        up = jnp.clip(gu[:, F:], -SWIGLU_LIMIT, SWIGLU_LIMIT)
        act = gate * (1.0 / (1.0 + jnp.exp(-SWIGLU_ALPHA * gate))) * (up + 1.0)
        y = _dot(act.astype(jnp.bfloat16), wd_bf[...]) + bd_ref[0]
        ybuf[slot] = y * roww_ref[0]

        def issue(i, carry):
            d = dst_ref[0, 0, i]
            pltpu.make_async_copy(ybuf.at[slot, pl.ds(i, 1)], y4_hbm.at[pl.ds(d, 1)], ssem.at[slot]).start()
            return carry
        lax.fori_loop(0, rows, issue, 0, unroll=8)

    @pl.when(b == nb - 1)
    def _():
        @pl.when(jnp.logical_and(b >= 1, b - 1 < nused))
        def _():
            wait_rows(ybuf.at[1 - slot], ssem.at[1 - slot])

        @pl.when(b < nused)
        def _():
            wait_rows(ybuf.at[slot], ssem.at[slot])


def _experts(nused, blk_exp, row_tok, row_dst, h2, row_w, wgu, bgu, wd, bd, n_out_rows):
    T, D = h2.shape
    nb = row_tok.shape[0]
    rows = row_tok.shape[2]
    F = wd.shape[1]
    grid_spec = pltpu.PrefetchScalarGridSpec(
        num_scalar_prefetch=2,
        grid=(nb,),
        in_specs=[
            pl.BlockSpec((1, 1, rows), lambda b, nu, be: (b, 0, 0), memory_space=pltpu.SMEM),
            pl.BlockSpec((1, 1, rows), lambda b, nu, be: (jnp.minimum(b + 1, nb - 1), 0, 0), memory_space=pltpu.SMEM),
            pl.BlockSpec((1, 1, rows), lambda b, nu, be: (b, 0, 0), memory_space=pltpu.SMEM),
            pl.BlockSpec(memory_space=pl.ANY),
            pl.BlockSpec((1, rows, 1), lambda b, nu, be: (b, 0, 0)),
            pl.BlockSpec((1, D, 2 * F), lambda b, nu, be: (be[b], 0, 0)),
            pl.BlockSpec((1, 1, 2 * F), lambda b, nu, be: (be[b], 0, 0)),
            pl.BlockSpec((1, F, D), lambda b, nu, be: (be[b], 0, 0)),
            pl.BlockSpec((1, 1, D), lambda b, nu, be: (be[b], 0, 0)),
        ],
        out_specs=pl.BlockSpec(memory_space=pl.ANY),
        scratch_shapes=[
            pltpu.VMEM((2, rows, D), jnp.float32),
            pltpu.VMEM((2, rows, D), jnp.float32),
            pltpu.VMEM((D, 2 * F), jnp.bfloat16),
            pltpu.VMEM((F, D), jnp.bfloat16),
            pltpu.SemaphoreType.DMA((2,)),
            pltpu.SemaphoreType.DMA((2,)),
        ],
    )
    return pl.pallas_call(
        _experts_kernel,
        grid_spec=grid_spec,
        out_shape=jax.ShapeDtypeStruct((n_out_rows, D), jnp.float32),
        compiler_params=pltpu.CompilerParams(
            dimension_semantics=("arbitrary",), vmem_limit_bytes=VMEM_LIMIT),
        name="experts",
    )(nused, blk_exp, row_tok, row_tok, row_dst, h2, row_w, wgu, bgu, wd, bd)


def _combine_kernel(x1_ref, y0_ref, y1_ref, y2_ref, y3_ref, o_ref):
    o_ref[...] = x1_ref[...] + (((y0_ref[...] + y1_ref[...]) + y2_ref[...]) + y3_ref[...])


def _combine(x1, y4):
    T, D = x1.shape
    tm = TOKEN_TILE
    nt = T // tm
    return pl.pallas_call(
        _combine_kernel,
        grid=(nt,),
        in_specs=[pl.BlockSpec((tm, D), lambda i: (i, 0))]
        + [pl.BlockSpec((tm, D), functools.partial(lambda i, k: (k * nt + i, 0), k=k)) for k in range(TOP_K)],
        out_specs=pl.BlockSpec((tm, D), lambda i: (i, 0)),
        out_shape=jax.ShapeDtypeStruct((T, D), jnp.float32),
        name="combine",
    )(x1, y4, y4, y4, y4)


def _rope_tables(seq):
    inv_freq = 1.0 / (ROPE_THETA ** (jnp.arange(0, HEAD_DIM, 2, dtype=jnp.float32) / HEAD_DIM))
    ang = jnp.arange(seq, dtype=jnp.float32)[:, None] * inv_freq[None, :]
    return jnp.cos(ang), jnp.sin(ang)


def _dispatch(top_idx, gate_w, rows):
    T = top_idx.shape[0]
    A = T * TOP_K
    e_flat = top_idx.reshape(A)
    order = jnp.argsort(e_flat)
    e_sorted = e_flat[order]
    counts = jnp.bincount(e_flat, length=N_EXPERTS).astype(jnp.int32)
    padded = ((counts + rows - 1) // rows) * rows
    pend = jnp.cumsum(padded)
    pstart = pend - padded
    ustart = jnp.cumsum(counts) - counts
    dest = pstart[e_sorted] + (jnp.arange(A, dtype=jnp.int32) - ustart[e_sorted])
    P = A + N_EXPERTS * rows
    nb = P // rows
    tok_sorted = (order // TOP_K).astype(jnp.int32)
    slot_sorted = (order % TOP_K).astype(jnp.int32)
    row_tok = jnp.zeros((P,), jnp.int32).at[dest].set(tok_sorted)
    p_idx = jnp.arange(P, dtype=jnp.int32)
    row_dst = (A + ((p_idx // rows) % 2) * rows + p_idx % rows).at[dest].set(slot_sorted * T + tok_sorted)
    row_w = jnp.zeros((P,), jnp.float32).at[dest].set(gate_w.reshape(A)[order])
    blk_exp = jnp.minimum(
        jnp.searchsorted(pend, jnp.arange(nb, dtype=jnp.int32) * rows, side='right'), N_EXPERTS - 1).astype(jnp.int32)
    nused = (pend[-1] // rows).astype(jnp.int32).reshape(1)
    return (nused, blk_exp, row_tok.reshape(nb, 1, rows), row_dst.reshape(nb, 1, rows),
            row_w.reshape(nb, rows, 1), A + 2 * rows)


def kernel(x, mem, norm1, w_in, b_gate, q_norm, k_norm, lambda_q1, lambda_k1, lambda_q2, lambda_k2, diff_subln,
           w_pool, pool_scale, mem_norm, w_mem_kv, mq_norm, mk_norm, w_br_diff, w_br_pool, w_br_mem, w_out, norm2,
           w_router, b_router, w_gate_up, b_gate_up, w_down, b_down):
    B, S, D = x.shape
    depth = norm1.shape[0]
    bf = jnp.bfloat16
    qw = 2 * DIFF_HEADS * HEAD_DIM
    vw = DIFF_HEADS * 2 * HEAD_DIM
    pw = w_pool.shape[1] * w_pool.shape[2]
    mw = MEM_HEADS * HEAD_DIM
    k_off, v_off, pool_off, mq_off, gate_off = qw, 2 * qw, 2 * qw + vw, 2 * qw + vw + pw, 2 * qw + vw + pw + mw
    half = HEAD_DIM // 2

    hpd = (jnp.arange(2)[None, :, None] * (DIFF_HEADS * HEAD_DIM) + jnp.arange(DIFF_HEADS)[:, None, None] * HEAD_DIM
           + jnp.arange(HEAD_DIM)[None, None, :])
    perm = hpd.reshape(-1)
    perm_swapped = (hpd - hpd % HEAD_DIM + (hpd % HEAD_DIM + half) % HEAD_DIM).reshape(-1)
    cos, sin = _rope_tables(S)
    cos_t, sin_t = cos.T, sin.T
    cos_k = jnp.tile(cos, (1, 4))
    sin_k = jnp.tile(jnp.concatenate([-sin, sin], axis=1), (1, 2))
    gidx = jnp.arange(qw) // HEAD_DIM
    gmat = (gidx[:, None] == gidx[None, :]).astype(bf)
    pidx = jnp.arange(pw) // w_pool.shape[2]

    for l in range(depth):
        wi = w_in[l]
        wt = jnp.concatenate([wi[:, perm], wi[:, v_off:pool_off], wi[:, mq_off:gate_off]], axis=1).T.astype(bf)
        wr = jnp.concatenate([wi[:, k_off + perm], wi[:, k_off + perm_swapped], wi[:, pool_off:mq_off]],
                             axis=1).astype(bf)
        wg = wi[:, gate_off:].astype(bf)
        kg = jnp.tile(k_norm[l], 2)[None, :]
        kgs = jnp.tile(jnp.roll(k_norm[l], half), 2)[None, :]
        wpool_bd = jnp.where(pidx[:, None] == pidx[None, :],
                             jnp.tile(w_pool[l].reshape(pw, -1), (1, w_pool.shape[1])), 0.0).astype(bf)
        mk, mv_t = _mem_kv(mem, mem_norm[l][None, :], w_mem_kv[l].T.astype(bf), mk_norm[l][:, None])
        q_t, k, v_t, y_b, y_c = _in_proj(
            x, norm1[l][None, :], wt, wr, gmat, cos_t, sin_t, cos_k, sin_k, q_norm[l][:, None], kg, kgs,
            mq_norm[l][:, None], wpool_bd, pool_scale[l][None, :], mk, mv_t)
        y_a = _diff_attn(q_t, k, v_t, lambda_q1[l][None, :], lambda_k1[l][None, :], lambda_q2[l][None, :],
                         lambda_k2[l][None, :], diff_subln[l][:, None])
        wr_t = w_router[l].T
        wr_hi = wr_t.astype(bf)
        wr_lo = (wr_t - wr_hi.astype(jnp.float32)).astype(bf)
        x1, h2, top_idx, gate_w = _merge(
            x, norm1[l][None, :], wg, b_gate[l][None, :], y_a, y_b, y_c, w_br_diff[l].astype(bf),
            w_br_pool[l].astype(bf), w_br_mem[l].astype(bf), w_out[l].astype(bf), norm2[l][None, :], wr_hi, wr_lo,
            b_router[l][:, None])
        T = B * S
        top_idx = top_idx.transpose(0, 2, 1).reshape(T, TOP_K)
        gate_w = gate_w.transpose(0, 2, 1).reshape(T, TOP_K)
        nused, blk_exp, row_tok, row_dst, row_w, P = _dispatch(top_idx, gate_w, EXPERT_ROWS)
        y4 = _experts(nused, blk_exp, row_tok, row_dst, h2.reshape(T, D), row_w, w_gate_up[l],
                      b_gate_up[l][:, None, :], w_down[l], b_down[l][:, None, :], P)
        x = _combine(x1.reshape(T, D), y4).reshape(B, S, D)
    return x
```

```python
import functools
import math

import jax
import jax.numpy as jnp
from jax import lax
from jax.experimental import pallas as pl
from jax.experimental.pallas import tpu as pltpu

HEAD_DIM = 64
CHUNK = 64
RMS_EPS = 1e-6
ROPE_THETA = 10000.0
DIFF_HEADS = 4
MEM_HEADS = 4
POOL_WINDOWS = (2, 4, 8, 16)
POOL_HALO = 16
N_EXPERTS = 32
TOP_K = 4
SWIGLU_LIMIT = 7.0
SWIGLU_ALPHA = 1.702
NEG_INF = -1e30
LAM_INIT = 0.8 - 0.6 * math.exp(-0.3 * 0)
LOG2E = math.log2(math.e)
FIXED_OFFSET_LIMIT = 40.0 * LOG2E

TOKEN_TILE = 512
EXPERT_ROWS = 256
VMEM_LIMIT = 56 * 1024 * 1024

_NT = (((1,), (1,)), ((), ()))


def _dot(a, b):
    return jnp.dot(a, b, preferred_element_type=jnp.float32)


def _dot_nt(a, b):
    return lax.dot_general(a, b, _NT, preferred_element_type=jnp.float32)


def _rms_rows(x, gain_row):
    return x * lax.rsqrt(jnp.mean(x * x, axis=-1, keepdims=True) + RMS_EPS) * gain_row


def _rms_cols(x, gain_col):
    return x * lax.rsqrt(jnp.mean(x * x, axis=0, keepdims=True) + RMS_EPS) * gain_col


def _mem_kv_kernel(mem_ref, gain_ref, wkv_t_ref, mk_gain_ref, mk_ref, mv_t_ref):
    mem_n = _rms_rows(mem_ref[0], gain_ref[...]).astype(jnp.bfloat16)
    kv_t = _dot_nt(wkv_t_ref[...], mem_n)
    mw = kv_t.shape[0] // 2
    for h in range(MEM_HEADS):
        blk = kv_t[h * HEAD_DIM:(h + 1) * HEAD_DIM]
        mk_t = _rms_cols(blk, mk_gain_ref[...]) * (HEAD_DIM ** -0.5)
        mk_ref[0, h] = mk_t.T.astype(jnp.bfloat16)
    mv_t_ref[0] = kv_t[mw:].astype(jnp.bfloat16)


def _mem_kv(mem, mem_norm, wkv_t, mk_gain_col):
    B, M, D = mem.shape
    mw = wkv_t.shape[0] // 2
    return pl.pallas_call(
        _mem_kv_kernel,
        grid=(B,),
        in_specs=[
            pl.BlockSpec((1, M, D), lambda b: (b, 0, 0)),
            pl.BlockSpec((1, D), lambda b: (0, 0)),
            pl.BlockSpec((2 * mw, D), lambda b: (0, 0)),
            pl.BlockSpec((HEAD_DIM, 1), lambda b: (0, 0)),
        ],
        out_specs=[
            pl.BlockSpec((1, MEM_HEADS, M, HEAD_DIM), lambda b: (b, 0, 0, 0)),
            pl.BlockSpec((1, mw, M), lambda b: (b, 0, 0)),
        ],
        out_shape=[
            jax.ShapeDtypeStruct((B, MEM_HEADS, M, HEAD_DIM), jnp.bfloat16),
            jax.ShapeDtypeStruct((B, mw, M), jnp.bfloat16),
        ],
        name="mem_kv",
    )(mem, mem_norm, wkv_t, mk_gain_col)


def _in_proj_kernel(x_ref, norm1_ref, wt_ref, wr_ref, gmat_ref, cos_t_ref, sin_t_ref, cos_k_ref, sin_k_ref,
                    qg_ref, kg_ref, kgs_ref, mqg_ref, wpool_ref, pscale_ref, mk_ref, mv_t_ref,
                    q_t_ref, k_ref, v_t_ref, yb_ref, yc_ref, halo_ref, ext_ref):
    s_idx = pl.program_id(1)
    tm = x_ref.shape[1]
    h = _rms_rows(x_ref[0], norm1_ref[...]).astype(jnp.bfloat16)
    z_t = _dot_nt(wt_ref[...], h)
    z_r = _dot(h, wr_ref[...])

    cos_t, sin_t = cos_t_ref[...], sin_t_ref[...]
    half = HEAD_DIM // 2
    for g in range(2 * DIFF_HEADS):
        y = _rms_cols(z_t[g * HEAD_DIM:(g + 1) * HEAD_DIM], qg_ref[...]) * (HEAD_DIM ** -0.5 * LOG2E)
        t1, t2 = y[:half], y[half:]
        q_t_ref[0, g * HEAD_DIM:g * HEAD_DIM + half] = (t1 * cos_t - t2 * sin_t).astype(jnp.bfloat16)
        q_t_ref[0, g * HEAD_DIM + half:(g + 1) * HEAD_DIM] = (t2 * cos_t + t1 * sin_t).astype(jnp.bfloat16)

    qw = 2 * DIFF_HEADS * HEAD_DIM
    vd = 2 * HEAD_DIM
    for hd in range(DIFF_HEADS):
        v_t_ref[0, hd, 0] = z_t[qw + hd * vd:qw + (hd + 1) * vd].astype(jnp.bfloat16)

    mq_off = qw + DIFF_HEADS * vd
    outs = []
    for hd in range(MEM_HEADS):
        mq_t = _rms_cols(z_t[mq_off + hd * HEAD_DIM:mq_off + (hd + 1) * HEAD_DIM], mqg_ref[...])
        s_t = _dot(mk_ref[0, hd], mq_t.astype(jnp.bfloat16))
        p_t = jnp.exp(s_t - jnp.max(s_t, axis=0, keepdims=True))
        o_t = _dot(mv_t_ref[0, hd * HEAD_DIM:(hd + 1) * HEAD_DIM], p_t.astype(jnp.bfloat16))
        outs.append(o_t / jnp.sum(p_t, axis=0, keepdims=True))
    yc_ref[0] = jnp.concatenate(outs, axis=0).T.astype(jnp.bfloat16)

    kw = 2 * DIFF_HEADS * HEAD_DIM
    zk, zks = z_r[:, :kw], z_r[:, kw:2 * kw]
    ssq = _dot((zk * zk).astype(jnp.bfloat16), gmat_ref[...])
    r = lax.rsqrt(ssq * (1.0 / HEAD_DIM) + RMS_EPS)
    cos_k, sin_k = cos_k_ref[...], sin_k_ref[...]
    for j in range(kw // 128):
        sl = slice(j * 128, (j + 1) * 128)
        kk = r[:, sl] * (zk[:, sl] * kg_ref[...] * cos_k + zks[:, sl] * kgs_ref[...] * sin_k)
        k_ref[0, :, sl] = kk.astype(jnp.bfloat16)

    u = z_r[:, 2 * kw:]

    @pl.when(s_idx == 0)
    def _():
        halo_ref[...] = jnp.zeros_like(halo_ref)

    ext_ref[0:POOL_HALO] = halo_ref[...]
    ext_ref[POOL_HALO:] = u
    halo_ref[...] = u[tm - POOL_HALO:]
    pos1 = (s_idx * tm + 1 + lax.broadcasted_iota(jnp.int32, (tm, 128), 0)).astype(jnp.float32)
    lane = lax.broadcasted_iota(jnp.int32, (tm, 128), 1)
    pooled = []
    for part in range(2):
        cols = slice(part * 128, (part + 1) * 128)
        w_a, w_b = POOL_WINDOWS[2 * part], POOL_WINDOWS[2 * part + 1]
        acc = ext_ref[POOL_HALO:POOL_HALO + tm, cols]
        sum_a = None
        for j in range(1, w_b):
            if j == w_a:
                sum_a = acc
            acc = acc + ext_ref[POOL_HALO - j:POOL_HALO - j + tm, cols]
        mean_a = sum_a / jnp.minimum(pos1, float(w_a))
        mean_b = acc / jnp.minimum(pos1, float(w_b))
        pooled.append(jnp.where(lane < 64, mean_a, mean_b) - u[:, cols])
    pooled = jnp.concatenate(pooled, axis=1).astype(jnp.bfloat16)
    yb_ref[0] = (_dot(pooled, wpool_ref[...]) * pscale_ref[...]).astype(jnp.bfloat16)


def _in_proj(x, norm1, wt, wr, gmat, cos_t, sin_t, cos_k, sin_k, qg, kg, kgs, mqg, wpool, pscale, mk, mv_t):
    B, S, D = x.shape
    tm = TOKEN_TILE
    ns = S // tm
    nt, nr = wt.shape[0], wr.shape[1]
    qw = 2 * DIFF_HEADS * HEAD_DIM
    vd = 2 * HEAD_DIM
    pw = wpool.shape[0]
    M = mk.shape[2]
    mw = mv_t.shape[1]
    const = lambda b, s: (0, 0)
    return pl.pallas_call(
        _in_proj_kernel,
        grid=(B, ns),
        in_specs=[
            pl.BlockSpec((1, tm, D), lambda b, s: (b, s, 0)),
            pl.BlockSpec((1, D), const),
            pl.BlockSpec((nt, D), const),
            pl.BlockSpec((D, nr), const),
            pl.BlockSpec((qw, qw), const),
            pl.BlockSpec((HEAD_DIM // 2, tm), lambda b, s: (0, s)),
            pl.BlockSpec((HEAD_DIM // 2, tm), lambda b, s: (0, s)),
            pl.BlockSpec((tm, 128), lambda b, s: (s, 0)),
            pl.BlockSpec((tm, 128), lambda b, s: (s, 0)),
            pl.BlockSpec((HEAD_DIM, 1), const),
            pl.BlockSpec((1, 128), const),
            pl.BlockSpec((1, 128), const),
            pl.BlockSpec((HEAD_DIM, 1), const),
            pl.BlockSpec((pw, pw), const),
            pl.BlockSpec((1, pw), const),
            pl.BlockSpec((1, MEM_HEADS, M, HEAD_DIM), lambda b, s: (b, 0, 0, 0)),
            pl.BlockSpec((1, mw, M), lambda b, s: (b, 0, 0)),
        ],
        out_specs=[
            pl.BlockSpec((1, qw, tm), lambda b, s: (b, 0, s)),
            pl.BlockSpec((1, tm, qw), lambda b, s: (b, s, 0)),
            pl.BlockSpec((1, DIFF_HEADS, 1, vd, tm), lambda b, s: (b, 0, s, 0, 0)),
            pl.BlockSpec((1, tm, pw), lambda b, s: (b, s, 0)),
            pl.BlockSpec((1, tm, mw), lambda b, s: (b, s, 0)),
        ],
        out_shape=[
            jax.ShapeDtypeStruct((B, qw, S), jnp.bfloat16),
            jax.ShapeDtypeStruct((B, S, qw), jnp.bfloat16),
            jax.ShapeDtypeStruct((B, DIFF_HEADS, ns, vd, tm), jnp.bfloat16),
            jax.ShapeDtypeStruct((B, S, pw), jnp.bfloat16),
            jax.ShapeDtypeStruct((B, S, mw), jnp.bfloat16),
        ],
        scratch_shapes=[
            pltpu.VMEM((POOL_HALO, pw), jnp.float32),
            pltpu.VMEM((POOL_HALO + tm, pw), jnp.float32),
        ],
        compiler_params=pltpu.CompilerParams(
            dimension_semantics=("arbitrary", "arbitrary"), vmem_limit_bytes=VMEM_LIMIT),
        name="in_proj",
    )(x, norm1, wt, wr, gmat, cos_t, sin_t, cos_k, sin_k, qg, kg, kgs, mqg, wpool, pscale, mk, mv_t)


def _absmax(ref):
    return lax.fori_loop(0, ref.shape[0], lambda i, m: jnp.maximum(m, jnp.abs(ref[i])), jnp.float32(0.0))


def _diff_attn_kernel(qn_ref, kn_ref, q_t_ref, k_ref, v_t_ref, lq1_ref, lk1_ref, lq2_ref, lk2_ref, subg_ref, o_ref,
                      rhs_ref, m_ref, l_ref, acc_ref):
    qi = pl.program_id(2)
    tq = q_t_ref.shape[2]
    tk = tq
    q_t = q_t_ref[0]
    row = lax.broadcasted_iota(jnp.int32, q_t.shape, 0)
    zero = jnp.zeros_like(q_t)
    rhs_ref[0] = jnp.where(row < HEAD_DIM, q_t, zero)
    rhs_ref[1] = jnp.where(row >= HEAD_DIM, q_t, zero)
    l_ref[...] = jnp.zeros(l_ref.shape, jnp.float32)
    acc_ref[...] = jnp.zeros(acc_ref.shape, jnp.float32)

    bound = (HEAD_DIM ** 0.5) * LOG2E * _absmax(qn_ref) * _absmax(kn_ref)

    def blocks(j):
        k_blk = k_ref[0, pl.ds(pl.multiple_of(j * tk, tk), tk), :]
        v_blk = v_t_ref[0, 0, j]
        return k_blk, v_blk

    def chunk_mask():
        kc = lax.broadcasted_iota(jnp.int32, (tk, tq), 0) // CHUNK
        qc = lax.broadcasted_iota(jnp.int32, (tk, tq), 1) // CHUNK
        return kc <= qc

    def sweep(step):
        lax.fori_loop(0, qi, lambda j, c: (step(j, False), c)[1], 0)
        step(qi, True)

    @pl.when(bound <= FIXED_OFFSET_LIMIT)
    def _():
        def step(j, masked):
            k_blk, v_blk = blocks(j)
            mask = chunk_mask() if masked else None
            for c in range(2):
                p = jnp.exp2(_dot(k_blk, rhs_ref[c]) - bound)
                if masked:
                    p = jnp.where(mask, p, 0.0)
                l_ref[c] += jnp.sum(p.reshape(tk // 8, 8, tq), axis=0)
                acc_ref[c] += _dot(v_blk, p.astype(jnp.bfloat16))
        sweep(step)

    @pl.when(bound > FIXED_OFFSET_LIMIT)
    def _():
        m_ref[...] = jnp.full(m_ref.shape, NEG_INF, jnp.float32)

        def step(j, masked):
            k_blk, v_blk = blocks(j)
            mask = chunk_mask() if masked else None
            for c in range(2):
                s = _dot(k_blk, rhs_ref[c])
                if masked:
                    s = jnp.where(mask, s, NEG_INF)
                m_prev = m_ref[c]
                m_new = jnp.maximum(m_prev, jnp.max(s, axis=0, keepdims=True))
                alpha = jnp.exp2(m_prev - m_new)
                p = jnp.exp2(s - m_new)
                l_ref[c, 0:1] = alpha * l_ref[c, 0:1] + jnp.sum(p, axis=0, keepdims=True)
                acc_ref[c] = alpha * acc_ref[c] + _dot(v_blk, p.astype(jnp.bfloat16))
                m_ref[c] = m_new
        sweep(step)

    lam = (jnp.exp(jnp.sum(lq1_ref[...] * lk1_ref[...], axis=-1, keepdims=True))
           - jnp.exp(jnp.sum(lq2_ref[...] * lk2_ref[...], axis=-1, keepdims=True)) + LAM_INIT)
    l0 = jnp.sum(l_ref[0], axis=0, keepdims=True)
    l1 = jnp.sum(l_ref[1], axis=0, keepdims=True)
    o = acc_ref[0] / l0 - lam * (acc_ref[1] / l1)
    y = _rms_cols(o, subg_ref[...]) * (1.0 - LAM_INIT)
    o_ref[0] = y.T.astype(jnp.bfloat16)


def _diff_attn(q_norm, k_norm, q_t, k, v_t, lq1, lk1, lq2, lk2, subg_col):
    B, qw, S = q_t.shape
    tq = TOKEN_TILE
    nq = S // tq
    vd = 2 * HEAD_DIM
    lam_spec = pl.BlockSpec((1, HEAD_DIM), lambda b, h, i: (0, 0))
    smem = pl.BlockSpec(memory_space=pltpu.SMEM)
    return pl.pallas_call(
        _diff_attn_kernel,
        grid=(B, DIFF_HEADS, nq),
        in_specs=[
            smem, smem,
            pl.BlockSpec((1, vd, tq), lambda b, h, i: (b, h, i)),
            pl.BlockSpec((1, S, vd), lambda b, h, i: (b, 0, h)),
            pl.BlockSpec((1, 1, nq, vd, tq), lambda b, h, i: (b, h, 0, 0, 0)),
            lam_spec, lam_spec, lam_spec, lam_spec,
            pl.BlockSpec((vd, 1), lambda b, h, i: (0, 0)),
        ],
        out_specs=pl.BlockSpec((1, tq, vd), lambda b, h, i: (b, i, h)),
        out_shape=jax.ShapeDtypeStruct((B, S, DIFF_HEADS * vd), jnp.bfloat16),
        scratch_shapes=[
            pltpu.VMEM((2, vd, tq), jnp.bfloat16),
            pltpu.VMEM((2, 1, tq), jnp.float32),
            pltpu.VMEM((2, 8, tq), jnp.float32),
            pltpu.VMEM((2, vd, tq), jnp.float32),
        ],
        compiler_params=pltpu.CompilerParams(
            dimension_semantics=("arbitrary", "arbitrary", "arbitrary"), vmem_limit_bytes=VMEM_LIMIT),
        name="diff_attn",
    )(q_norm, k_norm, q_t, k, v_t, lq1, lk1, lq2, lk2, subg_col)


def _merge_kernel(x_ref, norm1_ref, wg_ref, bg_ref, ya_ref, yb_ref, yc_ref, wa_ref, wb_ref, wc_ref, wo_ref,
                  norm2_ref, wr_hi_ref, wr_lo_ref, br_ref, x1_ref, h2_ref, idx_ref, gw_ref):
    D = x_ref.shape[2]
    x = x_ref[0]
    h = _rms_rows(x, norm1_ref[...]).astype(jnp.bfloat16)
    merged = None
    for i, (y_ref, w_ref) in enumerate(((ya_ref, wa_ref), (yb_ref, wb_ref), (yc_ref, wc_ref))):
        gz = _dot(h, wg_ref[:, i * D:(i + 1) * D]) + bg_ref[:, i * D:(i + 1) * D]
        gate = 1.0 / (1.0 + jnp.exp(-gz))
        term = gate * _dot(y_ref[0], w_ref[...])
        merged = term if merged is None else merged + term
    x1 = x + _dot(merged.astype(jnp.bfloat16), wo_ref[...])
    x1_ref[0] = x1
    h2 = _rms_rows(x1, norm2_ref[...])
    h2_ref[0] = h2

    hi = h2.astype(jnp.bfloat16)
    lo = (h2 - hi.astype(jnp.float32)).astype(jnp.bfloat16)
    logits = (_dot_nt(wr_hi_ref[...], hi) + _dot_nt(wr_hi_ref[...], lo) + _dot_nt(wr_lo_ref[...], hi)
              + br_ref[...])
    eidx = lax.broadcasted_iota(jnp.int32, logits.shape, 0)
    vals = logits
    top_v, top_i = [], []
    for _ in range(TOP_K):
        mx = jnp.max(vals, axis=0, keepdims=True)
        ix = jnp.min(jnp.where(vals == mx, eidx, N_EXPERTS), axis=0, keepdims=True)
        top_v.append(mx)
        top_i.append(ix)
        vals = jnp.where(eidx == ix, -jnp.inf, vals)
    ex = [jnp.exp(v - top_v[0]) for v in top_v]
    den = ex[0] + ex[1] + ex[2] + ex[3]
    idx_ref[0] = jnp.concatenate(top_i, axis=0)
    gw_ref[0] = jnp.concatenate([e / den for e in ex], axis=0)


def _merge(x, norm1, wg, bg, ya, yb, yc, wa, wb, wc, wo, norm2, wr_hi, wr_lo, br_col):
    B, S, D = x.shape
    tm = TOKEN_TILE
    const = lambda b, s: (0, 0)
    tok = lambda b, s: (b, s, 0)
    full = lambda a: pl.BlockSpec(a.shape, const)
    return pl.pallas_call(
        _merge_kernel,
        grid=(B, S // tm),
        in_specs=[
            pl.BlockSpec((1, tm, D), tok), full(norm1), full(wg), full(bg),
            pl.BlockSpec((1, tm, ya.shape[2]), tok), pl.BlockSpec((1, tm, yb.shape[2]), tok),
            pl.BlockSpec((1, tm, yc.shape[2]), tok),
            full(wa), full(wb), full(wc), full(wo), full(norm2), full(wr_hi), full(wr_lo), full(br_col),
        ],
        out_specs=[
            pl.BlockSpec((1, tm, D), tok), pl.BlockSpec((1, tm, D), tok),
            pl.BlockSpec((1, TOP_K, tm), lambda b, s: (b, 0, s)),
            pl.BlockSpec((1, TOP_K, tm), lambda b, s: (b, 0, s)),
        ],
        out_shape=[
            jax.ShapeDtypeStruct((B, S, D), jnp.float32),
            jax.ShapeDtypeStruct((B, S, D), jnp.float32),
            jax.ShapeDtypeStruct((B, TOP_K, S), jnp.int32),
            jax.ShapeDtypeStruct((B, TOP_K, S), jnp.float32),
        ],
        compiler_params=pltpu.CompilerParams(
            dimension_semantics=("arbitrary", "arbitrary"), vmem_limit_bytes=VMEM_LIMIT),
        name="merge",
    )(x, norm1, wg, bg, ya, yb, yc, wa, wb, wc, wo, norm2, wr_hi, wr_lo, br_col)


def _experts_kernel(nused_ref, bexp_ref, tok_cur_ref, tok_nxt_ref, dst_ref, h2_hbm, roww_ref, wgu_ref, bgu_ref,
                    wd_ref, bd_ref, y4_hbm, xbuf, ybuf, wgu_bf, wd_bf, gsem, ssem):
    b = pl.program_id(0)
    nb = pl.num_programs(0)
    nused = nused_ref[0]
    rows = xbuf.shape[1]
    F = wd_ref.shape[1]
    slot = b % 2

    def gather(tok_ref, buf_slot):
        def issue(i, carry):
            t = tok_ref[0, 0, i]
            pltpu.make_async_copy(h2_hbm.at[pl.ds(t, 1)], xbuf.at[buf_slot, pl.ds(i, 1)], gsem.at[buf_slot]).start()
            return carry
        lax.fori_loop(0, rows, issue, 0, unroll=8)

    def wait_rows(buf, sem):
        pltpu.make_async_copy(buf, buf, sem).wait()

    @pl.when(b == 0)
    def _():
        gather(tok_cur_ref, 0)
        ybuf[...] = jnp.zeros_like(ybuf)
        n_real = y4_hbm.shape[0] - 2 * rows
        for s in range(2):
            spare = pltpu.make_async_copy(ybuf.at[s], y4_hbm.at[pl.ds(n_real + s * rows, rows)], ssem.at[s])
            spare.start()
            spare.wait()

    @pl.when(b + 1 < nused)
    def _():
        gather(tok_nxt_ref, 1 - slot)

    @pl.when(jnp.logical_and(b >= 2, b - 2 < nused))
    def _():
        wait_rows(ybuf.at[slot], ssem.at[slot])

    changed = jnp.logical_or(b == 0, bexp_ref[b] != bexp_ref[jnp.maximum(b - 1, 0)])

    @pl.when(jnp.logical_and(changed, b < nused))
    def _():
        chunk = 128
        def cast(i, carry):
            r = pl.ds(pl.multiple_of(i * chunk, chunk), chunk)
            wgu_bf[r, :] = wgu_ref[0, r, :].astype(jnp.bfloat16)
            wd_bf[r, :] = wd_ref[0, r, :].astype(jnp.bfloat16)
            return carry
        lax.fori_loop(0, wgu_bf.shape[0] // chunk, cast, 0)

    @pl.when(jnp.logical_or(b == 0, b < nused))
    def _():
        wait_rows(xbuf.at[slot], gsem.at[slot])

    @pl.when(b < nused)
    def _():
        x = xbuf[slot].astype(jnp.bfloat16)
        gu = _dot(x, wgu_bf[...]) + bgu_ref[0]
        gate = jnp.minimum(gu[:, :F], SWIGLU_LIMIT)
        up = jnp.clip(gu[:, F:], -SWIGLU_LIMIT, SWIGLU_LIMIT)
        act = gate * (1.0 / (1.0 + jnp.exp(-SWIGLU_ALPHA * gate))) * (up + 1.0)
        y = _dot(act.astype(jnp.bfloat16), wd_bf[...]) + bd_ref[0]
        ybuf[slot] = y * roww_ref[0]

        def issue(i, carry):
            d = dst_ref[0, 0, i]
            pltpu.make_async_copy(ybuf.at[slot, pl.ds(i, 1)], y4_hbm.at[pl.ds(d, 1)], ssem.at[slot]).start()
            return carry
        lax.fori_loop(0, rows, issue, 0, unroll=8)

    @pl.when(b == nb - 1)
    def _():
        @pl.when(jnp.logical_and(b >= 1, b - 1 < nused))
        def _():
            wait_rows(ybuf.at[1 - slot], ssem.at[1 - slot])

        @pl.when(b < nused)
        def _():
            wait_rows(ybuf.at[slot], ssem.at[slot])


def _experts(nused, blk_exp, row_tok, row_dst, h2, row_w, wgu, bgu, wd, bd, n_out_rows):
    T, D = h2.shape
    nb = row_tok.shape[0]
    rows = row_tok.shape[2]
    F = wd.shape[1]
    grid_spec = pltpu.PrefetchScalarGridSpec(
        num_scalar_prefetch=2,
        grid=(nb,),
        in_specs=[
            pl.BlockSpec((1, 1, rows), lambda b, nu, be: (b, 0, 0), memory_space=pltpu.SMEM),
            pl.BlockSpec((1, 1, rows), lambda b, nu, be: (jnp.minimum(b + 1, nb - 1), 0, 0), memory_space=pltpu.SMEM),
            pl.BlockSpec((1, 1, rows), lambda b, nu, be: (b, 0, 0), memory_space=pltpu.SMEM),
            pl.BlockSpec(memory_space=pl.ANY),
            pl.BlockSpec((1, rows, 1), lambda b, nu, be: (b, 0, 0)),
            pl.BlockSpec((1, D, 2 * F), lambda b, nu, be: (be[b], 0, 0)),
            pl.BlockSpec((1, 1, 2 * F), lambda b, nu, be: (be[b], 0, 0)),
            pl.BlockSpec((1, F, D), lambda b, nu, be: (be[b], 0, 0)),
            pl.BlockSpec((1, 1, D), lambda b, nu, be: (be[b], 0, 0)),
        ],
        out_specs=pl.BlockSpec(memory_space=pl.ANY),
        scratch_shapes=[
            pltpu.VMEM((2, rows, D), jnp.float32),
            pltpu.VMEM((2, rows, D), jnp.float32),
            pltpu.VMEM((D, 2 * F), jnp.bfloat16),
            pltpu.VMEM((F, D), jnp.bfloat16),
            pltpu.SemaphoreType.DMA((2,)),
            pltpu.SemaphoreType.DMA((2,)),
        ],
    )
    return pl.pallas_call(
        _experts_kernel,
        grid_spec=grid_spec,
        out_shape=jax.ShapeDtypeStruct((n_out_rows, D), jnp.float32),
        compiler_params=pltpu.CompilerParams(
            dimension_semantics=("arbitrary",), vmem_limit_bytes=VMEM_LIMIT),
        name="experts",
    )(nused, blk_exp, row_tok, row_tok, row_dst, h2, row_w, wgu, bgu, wd, bd)


def _combine_kernel(x1_ref, y0_ref, y1_ref, y2_ref, y3_ref, o_ref):
    o_ref[...] = x1_ref[...] + (((y0_ref[...] + y1_ref[...]) + y2_ref[...]) + y3_ref[...])


def _combine(x1, y4):
    T, D = x1.shape
    tm = TOKEN_TILE
    nt = T // tm
    return pl.pallas_call(
        _combine_kernel,
        grid=(nt,),
        in_specs=[pl.BlockSpec((tm, D), lambda i: (i, 0))]
        + [pl.BlockSpec((tm, D), functools.partial(lambda i, k: (k * nt + i, 0), k=k)) for k in range(TOP_K)],
        out_specs=pl.BlockSpec((tm, D), lambda i: (i, 0)),
        out_shape=jax.ShapeDtypeStruct((T, D), jnp.float32),
        name="combine",
    )(x1, y4, y4, y4, y4)


def _rope_tables(seq):
    inv_freq = 1.0 / (ROPE_THETA ** (jnp.arange(0, HEAD_DIM, 2, dtype=jnp.float32) / HEAD_DIM))
    ang = jnp.arange(seq, dtype=jnp.float32)[:, None] * inv_freq[None, :]
    return jnp.cos(ang), jnp.sin(ang)


def _dispatch(top_idx, gate_w, rows):
    T = top_idx.shape[0]
    A = T * TOP_K
    e_flat = top_idx.reshape(A)
    order = jnp.argsort(e_flat).astype(jnp.int32)
    experts = jnp.arange(N_EXPERTS, dtype=jnp.int32)
    counts = jnp.sum((e_flat[:, None] == experts[None, :]).astype(jnp.int32), axis=0)
    padded = ((counts + rows - 1) // rows) * rows
    pend = jnp.cumsum(padded)
    pstart = pend - padded
    ustart = jnp.cumsum(counts) - counts
    P = A + N_EXPERTS * rows
    nb = P // rows
    blk_start = jnp.arange(nb, dtype=jnp.int32) * rows
    blk_exp = jnp.minimum(jnp.sum((pend[None, :] <= blk_start[:, None]).astype(jnp.int32), axis=1), N_EXPERTS - 1)
    in_blk = jnp.arange(rows, dtype=jnp.int32)[None, :]
    j = blk_start[:, None] + in_blk - pstart[blk_exp][:, None]
    valid = j < counts[blk_exp][:, None]
    a = order[jnp.clip(ustart[blk_exp][:, None] + j, 0, A - 1)]
    tok, slot = a // TOP_K, a % TOP_K
    row_tok = jnp.where(valid, tok, 0)
    row_dst = jnp.where(valid, slot * T + tok, A + (jnp.arange(nb, dtype=jnp.int32)[:, None] % 2) * rows + in_blk)
    row_w = jnp.where(valid, gate_w.reshape(A)[a], 0.0)
    nused = (pend[-1] // rows).astype(jnp.int32).reshape(1)
    return (nused, blk_exp, row_tok.reshape(nb, 1, rows), row_dst.reshape(nb, 1, rows),
            row_w.reshape(nb, rows, 1), A + 2 * rows)


def kernel(x, mem, norm1, w_in, b_gate, q_norm, k_norm, lambda_q1, lambda_k1, lambda_q2, lambda_k2, diff_subln,
           w_pool, pool_scale, mem_norm, w_mem_kv, mq_norm, mk_norm, w_br_diff, w_br_pool, w_br_mem, w_out, norm2,
           w_router, b_router, w_gate_up, b_gate_up, w_down, b_down):
    B, S, D = x.shape
    depth = norm1.shape[0]
    bf = jnp.bfloat16
    qw = 2 * DIFF_HEADS * HEAD_DIM
    vw = DIFF_HEADS * 2 * HEAD_DIM
    pw = w_pool.shape[1] * w_pool.shape[2]
    mw = MEM_HEADS * HEAD_DIM
    k_off, v_off, pool_off, mq_off, gate_off = qw, 2 * qw, 2 * qw + vw, 2 * qw + vw + pw, 2 * qw + vw + pw + mw
    half = HEAD_DIM // 2

    hpd = (jnp.arange(2)[None, :, None] * (DIFF_HEADS * HEAD_DIM) + jnp.arange(DIFF_HEADS)[:, None, None] * HEAD_DIM
           + jnp.arange(HEAD_DIM)[None, None, :])
    perm = hpd.reshape(-1)
    perm_swapped = (hpd - hpd % HEAD_DIM + (hpd % HEAD_DIM + half) % HEAD_DIM).reshape(-1)
    cos, sin = _rope_tables(S)
    cos_t, sin_t = cos.T, sin.T
    cos_k = jnp.tile(cos, (1, 4))
    sin_k = jnp.tile(jnp.concatenate([-sin, sin], axis=1), (1, 2))
    gidx = jnp.arange(qw) // HEAD_DIM
    gmat = (gidx[:, None] == gidx[None, :]).astype(bf)
    pidx = jnp.arange(pw) // w_pool.shape[2]

    for l in range(depth):
        wi = w_in[l]
        wt = jnp.concatenate([wi[:, perm], wi[:, v_off:pool_off], wi[:, mq_off:gate_off]], axis=1).T.astype(bf)
        wr = jnp.concatenate([wi[:, k_off + perm], wi[:, k_off + perm_swapped], wi[:, pool_off:mq_off]],
                             axis=1).astype(bf)
        wg = wi[:, gate_off:].astype(bf)
        kg = jnp.tile(k_norm[l], 2)[None, :]
        kgs = jnp.tile(jnp.roll(k_norm[l], half), 2)[None, :]
        wpool_bd = jnp.where(pidx[:, None] == pidx[None, :],
                             jnp.tile(w_pool[l].reshape(pw, -1), (1, w_pool.shape[1])), 0.0).astype(bf)
        mk, mv_t = _mem_kv(mem, mem_norm[l][None, :], w_mem_kv[l].T.astype(bf), mk_norm[l][:, None])
        q_t, k, v_t, y_b, y_c = _in_proj(
            x, norm1[l][None, :], wt, wr, gmat, cos_t, sin_t, cos_k, sin_k, q_norm[l][:, None], kg, kgs,
            mq_norm[l][:, None], wpool_bd, pool_scale[l][None, :], mk, mv_t)
        y_a = _diff_attn(q_norm[l], k_norm[l], q_t, k, v_t, lambda_q1[l][None, :], lambda_k1[l][None, :], lambda_q2[l][None, :],
                         lambda_k2[l][None, :], diff_subln[l][:, None])
        wr_t = w_router[l].T
        wr_hi = wr_t.astype(bf)
        wr_lo = (wr_t - wr_hi.astype(jnp.float32)).astype(bf)
        x1, h2, top_idx, gate_w = _merge(
            x, norm1[l][None, :], wg, b_gate[l][None, :], y_a, y_b, y_c, w_br_diff[l].astype(bf),
            w_br_pool[l].astype(bf), w_br_mem[l].astype(bf), w_out[l].astype(bf), norm2[l][None, :], wr_hi, wr_lo,
            b_router[l][:, None])
        T = B * S
        top_idx = top_idx.transpose(0, 2, 1).reshape(T, TOP_K)
        gate_w = gate_w.transpose(0, 2, 1).reshape(T, TOP_K)
        nused, blk_exp, row_tok, row_dst, row_w, P = _dispatch(top_idx, gate_w, EXPERT_ROWS)
        y4 = _experts(nused, blk_exp, row_tok, row_dst, h2.reshape(T, D), row_w, w_gate_up[l],
                      b_gate_up[l][:, None, :], w_down[l], b_down[l][:, None, :], P)
        x = _combine(x1.reshape(T, D), y4).reshape(B, S, D)
    return x
```

```python
import functools
import math

import jax
import jax.numpy as jnp
from jax import lax
from jax.experimental import pallas as pl
from jax.experimental.pallas import tpu as pltpu

HEAD_DIM = 64
CHUNK = 64
RMS_EPS = 1e-6
ROPE_THETA = 10000.0
DIFF_HEADS = 4
MEM_HEADS = 4
POOL_WINDOWS = (2, 4, 8, 16)
POOL_HALO = 16
N_EXPERTS = 32
TOP_K = 4
SWIGLU_LIMIT = 7.0
SWIGLU_ALPHA = 1.702
NEG_INF = -1e30
LAM_INIT = 0.8 - 0.6 * math.exp(-0.3 * 0)
LOG2E = math.log2(math.e)
FIXED_OFFSET_LIMIT = 40.0 * LOG2E

SUBLANES, LANES = 8, 128
TOKEN_TILE = 512
EXPERT_ROWS = 256
VMEM_LIMIT = 56 * 1024 * 1024

_NT = (((1,), (1,)), ((), ()))


def _dot(a, b):
    return jnp.dot(a, b, preferred_element_type=jnp.float32)


def _dot_nt(a, b):
    return lax.dot_general(a, b, _NT, preferred_element_type=jnp.float32)


def _store_token_tiles(ref, x):
    n = x.shape[0]
    for c in range(SUBLANES):
        ref[pl.ds(c, n, stride=SUBLANES), :] = x[:, c * LANES:(c + 1) * LANES]


def _load_token_tiles(ref, n):
    return jnp.concatenate([ref[pl.ds(c, n, stride=SUBLANES), :] for c in range(SUBLANES)], axis=1)


def _rms_rows(x, gain_row):
    return x * lax.rsqrt(jnp.mean(x * x, axis=-1, keepdims=True) + RMS_EPS) * gain_row


def _rms_cols(x, gain_col):
    return x * lax.rsqrt(jnp.mean(x * x, axis=0, keepdims=True) + RMS_EPS) * gain_col


def _mem_kv_kernel(mem_ref, gain_ref, wkv_t_ref, mk_gain_ref, mk_ref, mv_t_ref):
    mem_n = _rms_rows(mem_ref[0], gain_ref[...]).astype(jnp.bfloat16)
    kv_t = _dot_nt(wkv_t_ref[...], mem_n)
    mw = kv_t.shape[0] // 2
    for h in range(MEM_HEADS):
        blk = kv_t[h * HEAD_DIM:(h + 1) * HEAD_DIM]
        mk_t = _rms_cols(blk, mk_gain_ref[...]) * (HEAD_DIM ** -0.5)
        mk_ref[0, h] = mk_t.T.astype(jnp.bfloat16)
    mv_t_ref[0] = kv_t[mw:].astype(jnp.bfloat16)


def _mem_kv(mem, mem_norm, wkv_t, mk_gain_col):
    B, M, D = mem.shape
    mw = wkv_t.shape[0] // 2
    return pl.pallas_call(
        _mem_kv_kernel,
        grid=(B,),
        in_specs=[
            pl.BlockSpec((1, M, D), lambda b: (b, 0, 0)),
            pl.BlockSpec((1, D), lambda b: (0, 0)),
            pl.BlockSpec((2 * mw, D), lambda b: (0, 0)),
            pl.BlockSpec((HEAD_DIM, 1), lambda b: (0, 0)),
        ],
        out_specs=[
            pl.BlockSpec((1, MEM_HEADS, M, HEAD_DIM), lambda b: (b, 0, 0, 0)),
            pl.BlockSpec((1, mw, M), lambda b: (b, 0, 0)),
        ],
        out_shape=[
            jax.ShapeDtypeStruct((B, MEM_HEADS, M, HEAD_DIM), jnp.bfloat16),
            jax.ShapeDtypeStruct((B, mw, M), jnp.bfloat16),
        ],
        name="mem_kv",
    )(mem, mem_norm, wkv_t, mk_gain_col)


def _in_proj_kernel(x_ref, norm1_ref, wt_ref, wr_ref, gmat_ref, cos_t_ref, sin_t_ref, cos_k_ref, sin_k_ref,
                    qg_ref, kg_ref, kgs_ref, mqg_ref, wpool_ref, pscale_ref, mk_ref, mv_t_ref,
                    q_t_ref, k_ref, v_t_ref, yb_ref, yc_ref, halo_ref, ext_ref):
    s_idx = pl.program_id(1)
    tm = x_ref.shape[1]
    h = _rms_rows(x_ref[0], norm1_ref[...]).astype(jnp.bfloat16)
    z_t = _dot_nt(wt_ref[...], h)
    z_r = _dot(h, wr_ref[...])

    cos_t, sin_t = cos_t_ref[...], sin_t_ref[...]
    half = HEAD_DIM // 2
    for g in range(2 * DIFF_HEADS):
        y = _rms_cols(z_t[g * HEAD_DIM:(g + 1) * HEAD_DIM], qg_ref[...]) * (HEAD_DIM ** -0.5 * LOG2E)
        t1, t2 = y[:half], y[half:]
        q_t_ref[0, g * HEAD_DIM:g * HEAD_DIM + half] = (t1 * cos_t - t2 * sin_t).astype(jnp.bfloat16)
        q_t_ref[0, g * HEAD_DIM + half:(g + 1) * HEAD_DIM] = (t2 * cos_t + t1 * sin_t).astype(jnp.bfloat16)

    qw = 2 * DIFF_HEADS * HEAD_DIM
    vd = 2 * HEAD_DIM
    for hd in range(DIFF_HEADS):
        v_t_ref[0, hd, 0] = z_t[qw + hd * vd:qw + (hd + 1) * vd].astype(jnp.bfloat16)

    mq_off = qw + DIFF_HEADS * vd
    outs = []
    for hd in range(MEM_HEADS):
        mq_t = _rms_cols(z_t[mq_off + hd * HEAD_DIM:mq_off + (hd + 1) * HEAD_DIM], mqg_ref[...])
        s_t = _dot(mk_ref[0, hd], mq_t.astype(jnp.bfloat16))
        p_t = jnp.exp(s_t - jnp.max(s_t, axis=0, keepdims=True))
        o_t = _dot(mv_t_ref[0, hd * HEAD_DIM:(hd + 1) * HEAD_DIM], p_t.astype(jnp.bfloat16))
        outs.append(o_t / jnp.sum(p_t, axis=0, keepdims=True))
    yc_ref[0] = jnp.concatenate(outs, axis=0).T.astype(jnp.bfloat16)

    kw = 2 * DIFF_HEADS * HEAD_DIM
    zk, zks = z_r[:, :kw], z_r[:, kw:2 * kw]
    ssq = _dot((zk * zk).astype(jnp.bfloat16), gmat_ref[...])
    r = lax.rsqrt(ssq * (1.0 / HEAD_DIM) + RMS_EPS)
    cos_k, sin_k = cos_k_ref[...], sin_k_ref[...]
    for j in range(kw // 128):
        sl = slice(j * 128, (j + 1) * 128)
        kk = r[:, sl] * (zk[:, sl] * kg_ref[...] * cos_k + zks[:, sl] * kgs_ref[...] * sin_k)
        k_ref[0, :, sl] = kk.astype(jnp.bfloat16)

    u = z_r[:, 2 * kw:]

    @pl.when(s_idx == 0)
    def _():
        halo_ref[...] = jnp.zeros_like(halo_ref)

    ext_ref[0:POOL_HALO] = halo_ref[...]
    ext_ref[POOL_HALO:] = u
    halo_ref[...] = u[tm - POOL_HALO:]
    pos1 = (s_idx * tm + 1 + lax.broadcasted_iota(jnp.int32, (tm, 128), 0)).astype(jnp.float32)
    lane = lax.broadcasted_iota(jnp.int32, (tm, 128), 1)
    pooled = []
    for part in range(2):
        cols = slice(part * 128, (part + 1) * 128)
        w_a, w_b = POOL_WINDOWS[2 * part], POOL_WINDOWS[2 * part + 1]
        acc = ext_ref[POOL_HALO:POOL_HALO + tm, cols]
        sum_a = None
        for j in range(1, w_b):
            if j == w_a:
                sum_a = acc
            acc = acc + ext_ref[POOL_HALO - j:POOL_HALO - j + tm, cols]
        mean_a = sum_a / jnp.minimum(pos1, float(w_a))
        mean_b = acc / jnp.minimum(pos1, float(w_b))
        pooled.append(jnp.where(lane < 64, mean_a, mean_b) - u[:, cols])
    pooled = jnp.concatenate(pooled, axis=1).astype(jnp.bfloat16)
    yb_ref[0] = (_dot(pooled, wpool_ref[...]) * pscale_ref[...]).astype(jnp.bfloat16)


def _in_proj(x, norm1, wt, wr, gmat, cos_t, sin_t, cos_k, sin_k, qg, kg, kgs, mqg, wpool, pscale, mk, mv_t):
    B, S, D = x.shape
    tm = TOKEN_TILE
    ns = S // tm
    nt, nr = wt.shape[0], wr.shape[1]
    qw = 2 * DIFF_HEADS * HEAD_DIM
    vd = 2 * HEAD_DIM
    pw = wpool.shape[0]
    M = mk.shape[2]
    mw = mv_t.shape[1]
    const = lambda b, s: (0, 0)
    return pl.pallas_call(
        _in_proj_kernel,
        grid=(B, ns),
        in_specs=[
            pl.BlockSpec((1, tm, D), lambda b, s: (b, s, 0)),
            pl.BlockSpec((1, D), const),
            pl.BlockSpec((nt, D), const),
            pl.BlockSpec((D, nr), const),
            pl.BlockSpec((qw, qw), const),
            pl.BlockSpec((HEAD_DIM // 2, tm), lambda b, s: (0, s)),
            pl.BlockSpec((HEAD_DIM // 2, tm), lambda b, s: (0, s)),
            pl.BlockSpec((tm, 128), lambda b, s: (s, 0)),
            pl.BlockSpec((tm, 128), lambda b, s: (s, 0)),
            pl.BlockSpec((HEAD_DIM, 1), const),
            pl.BlockSpec((1, 128), const),
            pl.BlockSpec((1, 128), const),
            pl.BlockSpec((HEAD_DIM, 1), const),
            pl.BlockSpec((pw, pw), const),
            pl.BlockSpec((1, pw), const),
            pl.BlockSpec((1, MEM_HEADS, M, HEAD_DIM), lambda b, s: (b, 0, 0, 0)),
            pl.BlockSpec((1, mw, M), lambda b, s: (b, 0, 0)),
        ],
        out_specs=[
            pl.BlockSpec((1, qw, tm), lambda b, s: (b, 0, s)),
            pl.BlockSpec((1, tm, qw), lambda b, s: (b, s, 0)),
            pl.BlockSpec((1, DIFF_HEADS, 1, vd, tm), lambda b, s: (b, 0, s, 0, 0)),
            pl.BlockSpec((1, tm, pw), lambda b, s: (b, s, 0)),
            pl.BlockSpec((1, tm, mw), lambda b, s: (b, s, 0)),
        ],
        out_shape=[
            jax.ShapeDtypeStruct((B, qw, S), jnp.bfloat16),
            jax.ShapeDtypeStruct((B, S, qw), jnp.bfloat16),
            jax.ShapeDtypeStruct((B, DIFF_HEADS, ns, vd, tm), jnp.bfloat16),
            jax.ShapeDtypeStruct((B, S, pw), jnp.bfloat16),
            jax.ShapeDtypeStruct((B, S, mw), jnp.bfloat16),
        ],
        scratch_shapes=[
            pltpu.VMEM((POOL_HALO, pw), jnp.float32),
            pltpu.VMEM((POOL_HALO + tm, pw), jnp.float32),
        ],
        compiler_params=pltpu.CompilerParams(
            dimension_semantics=("arbitrary", "arbitrary"), vmem_limit_bytes=VMEM_LIMIT),
        name="in_proj",
    )(x, norm1, wt, wr, gmat, cos_t, sin_t, cos_k, sin_k, qg, kg, kgs, mqg, wpool, pscale, mk, mv_t)


def _absmax(ref):
    return lax.fori_loop(0, ref.shape[0], lambda i, m: jnp.maximum(m, jnp.abs(ref[i])), jnp.float32(0.0))


def _diff_attn_kernel(qn_ref, kn_ref, q_t_ref, k_ref, v_t_ref, lq1_ref, lk1_ref, lq2_ref, lk2_ref, subg_ref, o_ref,
                      rhs_ref, m_ref, l_ref, acc_ref, sa_ref, sb_ref):
    qi = pl.program_id(2)
    tq = q_t_ref.shape[2]
    tk = tq
    q_t = q_t_ref[0]
    row = lax.broadcasted_iota(jnp.int32, q_t.shape, 0)
    zero = jnp.zeros_like(q_t)
    rhs_ref[0] = jnp.where(row < HEAD_DIM, q_t, zero)
    rhs_ref[1] = jnp.where(row >= HEAD_DIM, q_t, zero)
    l_ref[...] = jnp.zeros(l_ref.shape, jnp.float32)
    acc_ref[...] = jnp.zeros(acc_ref.shape, jnp.float32)

    bound = (HEAD_DIM ** 0.5) * LOG2E * _absmax(qn_ref) * _absmax(kn_ref)

    def blocks(j):
        k_blk = k_ref[0, pl.ds(pl.multiple_of(j * tk, tk), tk), :]
        v_blk = v_t_ref[0, 0, j]
        return k_blk, v_blk

    def chunk_mask():
        kc = lax.broadcasted_iota(jnp.int32, (tk, tq), 0) // CHUNK
        qc = lax.broadcasted_iota(jnp.int32, (tk, tq), 1) // CHUNK
        return kc <= qc

    def sweep(step):
        lax.fori_loop(0, qi, lambda j, c: (step(j, False), c)[1], 0)
        step(qi, True)

    @pl.when(bound <= FIXED_OFFSET_LIMIT)
    def _():
        def produce(j, s_ref):
            k_blk = k_ref[0, pl.ds(pl.multiple_of(j * tk, tk), tk), :]
            for c in range(2):
                s_ref[c] = _dot(k_blk, rhs_ref[c])

        def consume(j, s_ref, masked):
            v_blk = v_t_ref[0, 0, j]
            mask = chunk_mask() if masked else None
            for c in range(2):
                p = jnp.exp2(s_ref[c] - bound)
                if masked:
                    p = jnp.where(mask, p, 0.0)
                l_ref[c] += jnp.sum(p.reshape(tk // 8, 8, tq), axis=0)
                acc_ref[c] += _dot(v_blk, p.astype(jnp.bfloat16))

        produce(0, sa_ref)

        def pair(i, carry):
            produce(2 * i + 1, sb_ref)
            consume(2 * i, sa_ref, False)
            produce(2 * i + 2, sa_ref)
            consume(2 * i + 1, sb_ref, False)
            return carry

        lax.fori_loop(0, qi // 2, pair, 0)

        @pl.when(qi % 2 == 0)
        def _():
            consume(qi, sa_ref, True)

        @pl.when(qi % 2 == 1)
        def _():
            produce(qi, sb_ref)
            consume(qi - 1, sa_ref, False)
            consume(qi, sb_ref, True)

    @pl.when(bound > FIXED_OFFSET_LIMIT)
    def _():
        m_ref[...] = jnp.full(m_ref.shape, NEG_INF, jnp.float32)

        def step(j, masked):
            k_blk, v_blk = blocks(j)
            mask = chunk_mask() if masked else None
            for c in range(2):
                s = _dot(k_blk, rhs_ref[c])
                if masked:
                    s = jnp.where(mask, s, NEG_INF)
                m_prev = m_ref[c]
                m_new = jnp.maximum(m_prev, jnp.max(s, axis=0, keepdims=True))
                alpha = jnp.exp2(m_prev - m_new)
                p = jnp.exp2(s - m_new)
                l_ref[c, 0:1] = alpha * l_ref[c, 0:1] + jnp.sum(p, axis=0, keepdims=True)
                acc_ref[c] = alpha * acc_ref[c] + _dot(v_blk, p.astype(jnp.bfloat16))
                m_ref[c] = m_new
        sweep(step)

    lam = (jnp.exp(jnp.sum(lq1_ref[...] * lk1_ref[...], axis=-1, keepdims=True))
           - jnp.exp(jnp.sum(lq2_ref[...] * lk2_ref[...], axis=-1, keepdims=True)) + LAM_INIT)
    l0 = jnp.sum(l_ref[0], axis=0, keepdims=True)
    l1 = jnp.sum(l_ref[1], axis=0, keepdims=True)
    o = acc_ref[0] / l0 - lam * (acc_ref[1] / l1)
    y = _rms_cols(o, subg_ref[...]) * (1.0 - LAM_INIT)
    o_ref[0] = y.T.astype(jnp.bfloat16)


def _diff_attn(q_norm, k_norm, q_t, k, v_t, lq1, lk1, lq2, lk2, subg_col):
    B, qw, S = q_t.shape
    tq = TOKEN_TILE
    nq = S // tq
    vd = 2 * HEAD_DIM
    lam_spec = pl.BlockSpec((1, HEAD_DIM), lambda b, h, i: (0, 0))
    smem = pl.BlockSpec(memory_space=pltpu.SMEM)
    return pl.pallas_call(
        _diff_attn_kernel,
        grid=(B, DIFF_HEADS, nq),
        in_specs=[
            smem, smem,
            pl.BlockSpec((1, vd, tq), lambda b, h, i: (b, h, i)),
            pl.BlockSpec((1, S, vd), lambda b, h, i: (b, 0, h)),
            pl.BlockSpec((1, 1, nq, vd, tq), lambda b, h, i: (b, h, 0, 0, 0)),
            lam_spec, lam_spec, lam_spec, lam_spec,
            pl.BlockSpec((vd, 1), lambda b, h, i: (0, 0)),
        ],
        out_specs=pl.BlockSpec((1, tq, vd), lambda b, h, i: (b, i, h)),
        out_shape=jax.ShapeDtypeStruct((B, S, DIFF_HEADS * vd), jnp.bfloat16),
        scratch_shapes=[
            pltpu.VMEM((2, vd, tq), jnp.bfloat16),
            pltpu.VMEM((2, 1, tq), jnp.float32),
            pltpu.VMEM((2, 8, tq), jnp.float32),
            pltpu.VMEM((2, vd, tq), jnp.float32),
            pltpu.VMEM((2, tq, tq), jnp.float32),
            pltpu.VMEM((2, tq, tq), jnp.float32),
        ],
        compiler_params=pltpu.CompilerParams(
            dimension_semantics=("arbitrary", "arbitrary", "arbitrary"), vmem_limit_bytes=VMEM_LIMIT),
        name="diff_attn",
    )(q_norm, k_norm, q_t, k, v_t, lq1, lk1, lq2, lk2, subg_col)


def _merge_kernel(x_ref, norm1_ref, wg_ref, bg_ref, ya_ref, yb_ref, yc_ref, wa_ref, wb_ref, wc_ref, wo_ref,
                  norm2_ref, wr_hi_ref, wr_lo_ref, br_ref, x1_ref, h2_ref, idx_ref, gw_ref):
    D = x_ref.shape[2]
    x = x_ref[0]
    h = _rms_rows(x, norm1_ref[...]).astype(jnp.bfloat16)
    merged = None
    for i, (y_ref, w_ref) in enumerate(((ya_ref, wa_ref), (yb_ref, wb_ref), (yc_ref, wc_ref))):
        gz = _dot(h, wg_ref[:, i * D:(i + 1) * D]) + bg_ref[:, i * D:(i + 1) * D]
        gate = 1.0 / (1.0 + jnp.exp(-gz))
        term = gate * _dot(y_ref[0], w_ref[...])
        merged = term if merged is None else merged + term
    x1 = x + _dot(merged.astype(jnp.bfloat16), wo_ref[...])
    x1_ref[0] = x1
    h2 = _rms_rows(x1, norm2_ref[...])
    _store_token_tiles(h2_ref, h2)

    hi = h2.astype(jnp.bfloat16)
    lo = (h2 - hi.astype(jnp.float32)).astype(jnp.bfloat16)
    logits = (_dot_nt(wr_hi_ref[...], hi) + _dot_nt(wr_hi_ref[...], lo) + _dot_nt(wr_lo_ref[...], hi)
              + br_ref[...])
    eidx = lax.broadcasted_iota(jnp.int32, logits.shape, 0)
    vals = logits
    top_v, top_i = [], []
    for _ in range(TOP_K):
        mx = jnp.max(vals, axis=0, keepdims=True)
        ix = jnp.min(jnp.where(vals == mx, eidx, N_EXPERTS), axis=0, keepdims=True)
        top_v.append(mx)
        top_i.append(ix)
        vals = jnp.where(eidx == ix, -jnp.inf, vals)
    ex = [jnp.exp(v - top_v[0]) for v in top_v]
    den = ex[0] + ex[1] + ex[2] + ex[3]
    idx_ref[0] = jnp.concatenate(top_i, axis=0)
    gw_ref[0] = jnp.concatenate([e / den for e in ex], axis=0)


def _merge(x, norm1, wg, bg, ya, yb, yc, wa, wb, wc, wo, norm2, wr_hi, wr_lo, br_col):
    B, S, D = x.shape
    tm = TOKEN_TILE
    const = lambda b, s: (0, 0)
    tok = lambda b, s: (b, s, 0)
    full = lambda a: pl.BlockSpec(a.shape, const)
    return pl.pallas_call(
        _merge_kernel,
        grid=(B, S // tm),
        in_specs=[
            pl.BlockSpec((1, tm, D), tok), full(norm1), full(wg), full(bg),
            pl.BlockSpec((1, tm, ya.shape[2]), tok), pl.BlockSpec((1, tm, yb.shape[2]), tok),
            pl.BlockSpec((1, tm, yc.shape[2]), tok),
            full(wa), full(wb), full(wc), full(wo), full(norm2), full(wr_hi), full(wr_lo), full(br_col),
        ],
        out_specs=[
            pl.BlockSpec((1, tm, D), tok),
            pl.BlockSpec((tm * SUBLANES, LANES), lambda b, s: (b * (S // tm) + s, 0)),
            pl.BlockSpec((1, TOP_K, tm), lambda b, s: (b, 0, s)),
            pl.BlockSpec((1, TOP_K, tm), lambda b, s: (b, 0, s)),
        ],
        out_shape=[
            jax.ShapeDtypeStruct((B, S, D), jnp.float32),
            jax.ShapeDtypeStruct((B * S * SUBLANES, LANES), jnp.float32),
            jax.ShapeDtypeStruct((B, TOP_K, S), jnp.int32),
            jax.ShapeDtypeStruct((B, TOP_K, S), jnp.float32),
        ],
        compiler_params=pltpu.CompilerParams(
            dimension_semantics=("arbitrary", "arbitrary"), vmem_limit_bytes=VMEM_LIMIT),
        name="merge",
    )(x, norm1, wg, bg, ya, yb, yc, wa, wb, wc, wo, norm2, wr_hi, wr_lo, br_col)


def _experts_kernel(nused_ref, bexp_ref, tok_cur_ref, tok_nxt_ref, dst_ref, h2_hbm, roww_ref, wgu_ref, bgu_ref,
                    wd_ref, bd_ref, y4_hbm, xbuf, ybuf, wgu_bf, wd_bf, gsem, ssem):
    b = pl.program_id(0)
    nb = pl.num_programs(0)
    nused = nused_ref[0]
    rows = xbuf.shape[1] // SUBLANES
    F = wd_ref.shape[1]
    slot = b % 2

    def tile(ref, off):
        return ref.at[pl.ds(pl.multiple_of(off, SUBLANES), SUBLANES)]

    def gather(tok_ref, buf_slot):
        for i in range(rows):
            pltpu.make_async_copy(tile(h2_hbm, tok_ref[0, 0, i]), xbuf.at[buf_slot, pl.ds(i * SUBLANES, SUBLANES)],
                                  gsem.at[buf_slot]).start()

    def wait_rows(buf, sem):
        pltpu.make_async_copy(buf, buf, sem).wait()

    @pl.when(b == 0)
    def _():
        gather(tok_cur_ref, 0)
        ybuf[...] = jnp.zeros_like(ybuf)
        n_real = y4_hbm.shape[0] - 2 * rows * SUBLANES
        for s in range(2):
            spare = pltpu.make_async_copy(
                ybuf.at[s], y4_hbm.at[pl.ds(n_real + s * rows * SUBLANES, rows * SUBLANES)], ssem.at[s])
            spare.start()
            spare.wait()

    @pl.when(b + 1 < nused)
    def _():
        gather(tok_nxt_ref, 1 - slot)

    @pl.when(jnp.logical_and(b >= 2, b - 2 < nused))
    def _():
        wait_rows(ybuf.at[slot], ssem.at[slot])

    changed = jnp.logical_or(b == 0, bexp_ref[b] != bexp_ref[jnp.maximum(b - 1, 0)])

    @pl.when(jnp.logical_and(changed, b < nused))
    def _():
        chunk = 128
        def cast(i, carry):
            r = pl.ds(pl.multiple_of(i * chunk, chunk), chunk)
            wgu_bf[r, :] = wgu_ref[0, r, :].astype(jnp.bfloat16)
            wd_bf[r, :] = wd_ref[0, r, :].astype(jnp.bfloat16)
            return carry
        lax.fori_loop(0, wgu_bf.shape[0] // chunk, cast, 0)

    @pl.when(jnp.logical_or(b == 0, b < nused))
    def _():
        wait_rows(xbuf.at[slot], gsem.at[slot])

    @pl.when(b < nused)
    def _():
        x = _load_token_tiles(xbuf.at[slot], rows).astype(jnp.bfloat16)
        gu = _dot(x, wgu_bf[...]) + bgu_ref[0]
        gate = jnp.minimum(gu[:, :F], SWIGLU_LIMIT)
        up = jnp.clip(gu[:, F:], -SWIGLU_LIMIT, SWIGLU_LIMIT)
        act = gate * (1.0 / (1.0 + jnp.exp(-SWIGLU_ALPHA * gate))) * (up + 1.0)
        y = _dot(act.astype(jnp.bfloat16), wd_bf[...]) + bd_ref[0]
        _store_token_tiles(ybuf.at[slot], y * roww_ref[0])
        for i in range(rows):
            pltpu.make_async_copy(ybuf.at[slot, pl.ds(i * SUBLANES, SUBLANES)], tile(y4_hbm, dst_ref[0, 0, i]),
                                  ssem.at[slot]).start()

    @pl.when(b == nb - 1)
    def _():
        @pl.when(jnp.logical_and(b >= 1, b - 1 < nused))
        def _():
            wait_rows(ybuf.at[1 - slot], ssem.at[1 - slot])

        @pl.when(b < nused)
        def _():
            wait_rows(ybuf.at[slot], ssem.at[slot])


def _experts(nused, blk_exp, row_tok, row_dst, h2, row_w, wgu, bgu, wd, bd, n_out_rows):
    D = wgu.shape[1]
    nb = row_tok.shape[0]
    rows = row_tok.shape[2]
    F = wd.shape[1]
    grid_spec = pltpu.PrefetchScalarGridSpec(
        num_scalar_prefetch=2,
        grid=(nb,),
        in_specs=[
            pl.BlockSpec((1, 1, rows), lambda b, nu, be: (b, 0, 0), memory_space=pltpu.SMEM),
            pl.BlockSpec((1, 1, rows), lambda b, nu, be: (jnp.minimum(b + 1, nb - 1), 0, 0), memory_space=pltpu.SMEM),
            pl.BlockSpec((1, 1, rows), lambda b, nu, be: (b, 0, 0), memory_space=pltpu.SMEM),
            pl.BlockSpec(memory_space=pl.ANY),
            pl.BlockSpec((1, rows, 1), lambda b, nu, be: (b, 0, 0)),
            pl.BlockSpec((1, D, 2 * F), lambda b, nu, be: (be[b], 0, 0)),
            pl.BlockSpec((1, 1, 2 * F), lambda b, nu, be: (be[b], 0, 0)),
            pl.BlockSpec((1, F, D), lambda b, nu, be: (be[b], 0, 0)),
            pl.BlockSpec((1, 1, D), lambda b, nu, be: (be[b], 0, 0)),
        ],
        out_specs=pl.BlockSpec(memory_space=pl.ANY),
        scratch_shapes=[
            pltpu.VMEM((2, rows * SUBLANES, LANES), jnp.float32),
            pltpu.VMEM((2, rows * SUBLANES, LANES), jnp.float32),
            pltpu.VMEM((D, 2 * F), jnp.bfloat16),
            pltpu.VMEM((F, D), jnp.bfloat16),
            pltpu.SemaphoreType.DMA((2,)),
            pltpu.SemaphoreType.DMA((2,)),
        ],
    )
    return pl.pallas_call(
        _experts_kernel,
        grid_spec=grid_spec,
        out_shape=jax.ShapeDtypeStruct((n_out_rows * SUBLANES, LANES), jnp.float32),
        compiler_params=pltpu.CompilerParams(
            dimension_semantics=("arbitrary",), vmem_limit_bytes=VMEM_LIMIT),
        name="experts",
    )(nused, blk_exp, row_tok, row_tok, row_dst, h2, row_w, wgu, bgu, wd, bd)


def _combine_kernel(x1_ref, y0_ref, y1_ref, y2_ref, y3_ref, o_ref):
    tm = x1_ref.shape[0]
    for c in range(SUBLANES):
        rows = pl.ds(c, tm, stride=SUBLANES)
        cols = slice(c * LANES, (c + 1) * LANES)
        o_ref[:, cols] = x1_ref[:, cols] + (((y0_ref[rows, :] + y1_ref[rows, :]) + y2_ref[rows, :]) + y3_ref[rows, :])


def _combine(x1, y4):
    T, D = x1.shape
    tm = TOKEN_TILE
    nt = T // tm
    return pl.pallas_call(
        _combine_kernel,
        grid=(nt,),
        in_specs=[pl.BlockSpec((tm, D), lambda i: (i, 0))]
        + [pl.BlockSpec((tm * SUBLANES, LANES), functools.partial(lambda i, k: (k * nt + i, 0), k=k))
           for k in range(TOP_K)],
        out_specs=pl.BlockSpec((tm, D), lambda i: (i, 0)),
        out_shape=jax.ShapeDtypeStruct((T, D), jnp.float32),
        name="combine",
    )(x1, y4, y4, y4, y4)


def _rope_tables(seq):
    inv_freq = 1.0 / (ROPE_THETA ** (jnp.arange(0, HEAD_DIM, 2, dtype=jnp.float32) / HEAD_DIM))
    ang = jnp.arange(seq, dtype=jnp.float32)[:, None] * inv_freq[None, :]
    return jnp.cos(ang), jnp.sin(ang)


def _dispatch(top_idx, gate_w, rows):
    T = top_idx.shape[0]
    A = T * TOP_K
    e_flat = top_idx.reshape(A)
    order = jnp.argsort(e_flat).astype(jnp.int32)
    experts = jnp.arange(N_EXPERTS, dtype=jnp.int32)
    counts = jnp.sum((e_flat[:, None] == experts[None, :]).astype(jnp.int32), axis=0)
    padded = ((counts + rows - 1) // rows) * rows
    pend = jnp.cumsum(padded)
    pstart = pend - padded
    ustart = jnp.cumsum(counts) - counts
    P = A + N_EXPERTS * rows
    nb = P // rows
    blk_start = jnp.arange(nb, dtype=jnp.int32) * rows
    blk_exp = jnp.minimum(jnp.sum((pend[None, :] <= blk_start[:, None]).astype(jnp.int32), axis=1), N_EXPERTS - 1)
    in_blk = jnp.arange(rows, dtype=jnp.int32)[None, :]
    j = blk_start[:, None] + in_blk - pstart[blk_exp][:, None]
    valid = j < counts[blk_exp][:, None]
    a = order[jnp.clip(ustart[blk_exp][:, None] + j, 0, A - 1)]
    tok, slot = a // TOP_K, a % TOP_K
    row_tok = jnp.where(valid, tok, 0)
    row_dst = jnp.where(valid, slot * T + tok, A + (jnp.arange(nb, dtype=jnp.int32)[:, None] % 2) * rows + in_blk)
    row_w = jnp.where(valid, gate_w.reshape(A)[a], 0.0)
    nused = (pend[-1] // rows).astype(jnp.int32).reshape(1)
    return (nused, blk_exp, (row_tok * SUBLANES).reshape(nb, 1, rows), (row_dst * SUBLANES).reshape(nb, 1, rows),
            row_w.reshape(nb, rows, 1), A + 2 * rows)


def kernel(x, mem, norm1, w_in, b_gate, q_norm, k_norm, lambda_q1, lambda_k1, lambda_q2, lambda_k2, diff_subln,
           w_pool, pool_scale, mem_norm, w_mem_kv, mq_norm, mk_norm, w_br_diff, w_br_pool, w_br_mem, w_out, norm2,
           w_router, b_router, w_gate_up, b_gate_up, w_down, b_down):
    B, S, D = x.shape
    depth = norm1.shape[0]
    bf = jnp.bfloat16
    qw = 2 * DIFF_HEADS * HEAD_DIM
    vw = DIFF_HEADS * 2 * HEAD_DIM
    pw = w_pool.shape[1] * w_pool.shape[2]
    mw = MEM_HEADS * HEAD_DIM
    k_off, v_off, pool_off, mq_off, gate_off = qw, 2 * qw, 2 * qw + vw, 2 * qw + vw + pw, 2 * qw + vw + pw + mw
    half = HEAD_DIM // 2

    hpd = (jnp.arange(2)[None, :, None] * (DIFF_HEADS * HEAD_DIM) + jnp.arange(DIFF_HEADS)[:, None, None] * HEAD_DIM
           + jnp.arange(HEAD_DIM)[None, None, :])
    perm = hpd.reshape(-1)
    perm_swapped = (hpd - hpd % HEAD_DIM + (hpd % HEAD_DIM + half) % HEAD_DIM).reshape(-1)
    cos, sin = _rope_tables(S)
    cos_t, sin_t = cos.T, sin.T
    cos_k = jnp.tile(cos, (1, 4))
    sin_k = jnp.tile(jnp.concatenate([-sin, sin], axis=1), (1, 2))
    gidx = jnp.arange(qw) // HEAD_DIM
    gmat = (gidx[:, None] == gidx[None, :]).astype(bf)
    pidx = jnp.arange(pw) // w_pool.shape[2]

    for l in range(depth):
        wi = w_in[l]
        wt = jnp.concatenate([wi[:, perm], wi[:, v_off:pool_off], wi[:, mq_off:gate_off]], axis=1).T.astype(bf)
        wr = jnp.concatenate([wi[:, k_off + perm], wi[:, k_off + perm_swapped], wi[:, pool_off:mq_off]],
                             axis=1).astype(bf)
        wg = wi[:, gate_off:].astype(bf)
        kg = jnp.tile(k_norm[l], 2)[None, :]
        kgs = jnp.tile(jnp.roll(k_norm[l], half), 2)[None, :]
        wpool_bd = jnp.where(pidx[:, None] == pidx[None, :],
                             jnp.tile(w_pool[l].reshape(pw, -1), (1, w_pool.shape[1])), 0.0).astype(bf)
        mk, mv_t = _mem_kv(mem, mem_norm[l][None, :], w_mem_kv[l].T.astype(bf), mk_norm[l][:, None])
        q_t, k, v_t, y_b, y_c = _in_proj(
            x, norm1[l][None, :], wt, wr, gmat, cos_t, sin_t, cos_k, sin_k, q_norm[l][:, None], kg, kgs,
            mq_norm[l][:, None], wpool_bd, pool_scale[l][None, :], mk, mv_t)
        y_a = _diff_attn(q_norm[l], k_norm[l], q_t, k, v_t, lambda_q1[l][None, :], lambda_k1[l][None, :], lambda_q2[l][None, :],
                         lambda_k2[l][None, :], diff_subln[l][:, None])
        wr_t = w_router[l].T
        wr_hi = wr_t.astype(bf)
        wr_lo = (wr_t - wr_hi.astype(jnp.float32)).astype(bf)
        x1, h2, top_idx, gate_w = _merge(
            x, norm1[l][None, :], wg, b_gate[l][None, :], y_a, y_b, y_c, w_br_diff[l].astype(bf),
            w_br_pool[l].astype(bf), w_br_mem[l].astype(bf), w_out[l].astype(bf), norm2[l][None, :], wr_hi, wr_lo,
            b_router[l][:, None])
        T = B * S
        top_idx = top_idx.transpose(0, 2, 1).reshape(T, TOP_K)
        gate_w = gate_w.transpose(0, 2, 1).reshape(T, TOP_K)
        nused, blk_exp, row_tok, row_dst, row_w, P = _dispatch(top_idx, gate_w, EXPERT_ROWS)
        y4 = _experts(nused, blk_exp, row_tok, row_dst, h2, row_w, w_gate_up[l],
                      b_gate_up[l][:, None, :], w_down[l], b_down[l][:, None, :], P)
        x = _combine(x1.reshape(T, D), y4).reshape(B, S, D)
    return x
```

```python
import functools
import math

import jax
import jax.numpy as jnp
from jax import lax
from jax.experimental import pallas as pl
from jax.experimental.pallas import tpu as pltpu

HEAD_DIM = 64
CHUNK = 64
RMS_EPS = 1e-6
ROPE_THETA = 10000.0
DIFF_HEADS = 4
MEM_HEADS = 4
POOL_WINDOWS = (2, 4, 8, 16)
POOL_HALO = 16
N_EXPERTS = 32
TOP_K = 4
SWIGLU_LIMIT = 7.0
SWIGLU_ALPHA = 1.702
NEG_INF = -1e30
LAM_INIT = 0.8 - 0.6 * math.exp(-0.3 * 0)
LOG2E = math.log2(math.e)
FIXED_OFFSET_LIMIT = 40.0 * LOG2E

SUBLANES, LANES = 8, 128
TOKEN_TILE = 512
EXPERT_ROWS = 256
VMEM_LIMIT = 56 * 1024 * 1024

_NT = (((1,), (1,)), ((), ()))


def _dot(a, b):
    return jnp.dot(a, b, preferred_element_type=jnp.float32)


def _dot_nt(a, b):
    return lax.dot_general(a, b, _NT, preferred_element_type=jnp.float32)


def _store_token_tiles(ref, x):
    n = x.shape[0]
    for c in range(SUBLANES):
        ref[pl.ds(c, n, stride=SUBLANES), :] = x[:, c * LANES:(c + 1) * LANES]


def _load_token_tiles(ref, n):
    return jnp.concatenate([ref[pl.ds(c, n, stride=SUBLANES), :] for c in range(SUBLANES)], axis=1)


def _rms_rows(x, gain_row):
    return x * lax.rsqrt(jnp.mean(x * x, axis=-1, keepdims=True) + RMS_EPS) * gain_row


def _rms_cols(x, gain_col):
    return x * lax.rsqrt(jnp.mean(x * x, axis=0, keepdims=True) + RMS_EPS) * gain_col


def _mem_kv_kernel(mem_ref, gain_ref, wkv_t_ref, mk_gain_ref, mk_ref, mv_t_ref):
    mem_n = _rms_rows(mem_ref[0], gain_ref[...]).astype(jnp.bfloat16)
    kv_t = _dot_nt(wkv_t_ref[...], mem_n)
    mw = kv_t.shape[0] // 2
    for h in range(MEM_HEADS):
        blk = kv_t[h * HEAD_DIM:(h + 1) * HEAD_DIM]
        mk_t = _rms_cols(blk, mk_gain_ref[...]) * (HEAD_DIM ** -0.5)
        mk_ref[0, h] = mk_t.T.astype(jnp.bfloat16)
    mv_t_ref[0] = kv_t[mw:].astype(jnp.bfloat16)


def _mem_kv(mem, mem_norm, wkv_t, mk_gain_col):
    B, M, D = mem.shape
    mw = wkv_t.shape[0] // 2
    return pl.pallas_call(
        _mem_kv_kernel,
        grid=(B,),
        in_specs=[
            pl.BlockSpec((1, M, D), lambda b: (b, 0, 0)),
            pl.BlockSpec((1, D), lambda b: (0, 0)),
            pl.BlockSpec((2 * mw, D), lambda b: (0, 0)),
            pl.BlockSpec((HEAD_DIM, 1), lambda b: (0, 0)),
        ],
        out_specs=[
            pl.BlockSpec((1, MEM_HEADS, M, HEAD_DIM), lambda b: (b, 0, 0, 0)),
            pl.BlockSpec((1, mw, M), lambda b: (b, 0, 0)),
        ],
        out_shape=[
            jax.ShapeDtypeStruct((B, MEM_HEADS, M, HEAD_DIM), jnp.bfloat16),
            jax.ShapeDtypeStruct((B, mw, M), jnp.bfloat16),
        ],
        name="mem_kv",
    )(mem, mem_norm, wkv_t, mk_gain_col)


def _in_proj_kernel(x_ref, norm1_ref, wt_ref, wr_ref, gmat_ref, cos_t_ref, sin_t_ref, cos_k_ref, sin_k_ref,
                    qg_ref, kg_ref, kgs_ref, mqg_ref, wpool_ref, pscale_ref, mk_ref, mv_t_ref,
                    q_t_ref, k_ref, v_t_ref, yb_ref, yc_ref, halo_ref, ext_ref):
    s_idx = pl.program_id(1)
    tm = x_ref.shape[1]
    h = _rms_rows(x_ref[0], norm1_ref[...]).astype(jnp.bfloat16)
    z_t = _dot_nt(wt_ref[...], h)
    z_r = _dot(h, wr_ref[...])

    cos_t, sin_t = cos_t_ref[...], sin_t_ref[...]
    half = HEAD_DIM // 2
    for g in range(2 * DIFF_HEADS):
        y = _rms_cols(z_t[g * HEAD_DIM:(g + 1) * HEAD_DIM], qg_ref[...]) * (HEAD_DIM ** -0.5 * LOG2E)
        t1, t2 = y[:half], y[half:]
        q_t_ref[0, g * HEAD_DIM:g * HEAD_DIM + half] = (t1 * cos_t - t2 * sin_t).astype(jnp.bfloat16)
        q_t_ref[0, g * HEAD_DIM + half:(g + 1) * HEAD_DIM] = (t2 * cos_t + t1 * sin_t).astype(jnp.bfloat16)

    qw = 2 * DIFF_HEADS * HEAD_DIM
    vd = 2 * HEAD_DIM
    for hd in range(DIFF_HEADS):
        v_t_ref[0, hd, 0] = z_t[qw + hd * vd:qw + (hd + 1) * vd].astype(jnp.bfloat16)

    mq_off = qw + DIFF_HEADS * vd
    outs = []
    for hd in range(MEM_HEADS):
        mq_t = _rms_cols(z_t[mq_off + hd * HEAD_DIM:mq_off + (hd + 1) * HEAD_DIM], mqg_ref[...])
        s_t = _dot(mk_ref[0, hd], mq_t.astype(jnp.bfloat16))
        p_t = jnp.exp(s_t - jnp.max(s_t, axis=0, keepdims=True))
        o_t = _dot(mv_t_ref[0, hd * HEAD_DIM:(hd + 1) * HEAD_DIM], p_t.astype(jnp.bfloat16))
        outs.append(o_t / jnp.sum(p_t, axis=0, keepdims=True))
    yc_ref[0] = jnp.concatenate(outs, axis=0).T.astype(jnp.bfloat16)

    kw = 2 * DIFF_HEADS * HEAD_DIM
    zk, zks = z_r[:, :kw], z_r[:, kw:2 * kw]
    ssq = _dot((zk * zk).astype(jnp.bfloat16), gmat_ref[...])
    r = lax.rsqrt(ssq * (1.0 / HEAD_DIM) + RMS_EPS)
    cos_k, sin_k = cos_k_ref[...], sin_k_ref[...]
    for j in range(kw // 128):
        sl = slice(j * 128, (j + 1) * 128)
        kk = r[:, sl] * (zk[:, sl] * kg_ref[...] * cos_k + zks[:, sl] * kgs_ref[...] * sin_k)
        k_ref[0, :, sl] = kk.astype(jnp.bfloat16)

    u = z_r[:, 2 * kw:]

    @pl.when(s_idx == 0)
    def _():
        halo_ref[...] = jnp.zeros_like(halo_ref)

    ext_ref[0:POOL_HALO] = halo_ref[...]
    ext_ref[POOL_HALO:] = u
    halo_ref[...] = u[tm - POOL_HALO:]
    pos1 = (s_idx * tm + 1 + lax.broadcasted_iota(jnp.int32, (tm, 128), 0)).astype(jnp.float32)
    lane = lax.broadcasted_iota(jnp.int32, (tm, 128), 1)
    pooled = []
    for part in range(2):
        cols = slice(part * 128, (part + 1) * 128)
        w_a, w_b = POOL_WINDOWS[2 * part], POOL_WINDOWS[2 * part + 1]
        acc = ext_ref[POOL_HALO:POOL_HALO + tm, cols]
        sum_a = None
        for j in range(1, w_b):
            if j == w_a:
                sum_a = acc
            acc = acc + ext_ref[POOL_HALO - j:POOL_HALO - j + tm, cols]
        mean_a = sum_a / jnp.minimum(pos1, float(w_a))
        mean_b = acc / jnp.minimum(pos1, float(w_b))
        pooled.append(jnp.where(lane < 64, mean_a, mean_b) - u[:, cols])
    pooled = jnp.concatenate(pooled, axis=1).astype(jnp.bfloat16)
    yb_ref[0] = (_dot(pooled, wpool_ref[...]) * pscale_ref[...]).astype(jnp.bfloat16)


def _in_proj(x, norm1, wt, wr, gmat, cos_t, sin_t, cos_k, sin_k, qg, kg, kgs, mqg, wpool, pscale, mk, mv_t):
    B, S, D = x.shape
    tm = TOKEN_TILE
    ns = S // tm
    nt, nr = wt.shape[0], wr.shape[1]
    qw = 2 * DIFF_HEADS * HEAD_DIM
    vd = 2 * HEAD_DIM
    pw = wpool.shape[0]
    M = mk.shape[2]
    mw = mv_t.shape[1]
    const = lambda b, s: (0, 0)
    return pl.pallas_call(
        _in_proj_kernel,
        grid=(B, ns),
        in_specs=[
            pl.BlockSpec((1, tm, D), lambda b, s: (b, s, 0)),
            pl.BlockSpec((1, D), const),
            pl.BlockSpec((nt, D), const),
            pl.BlockSpec((D, nr), const),
            pl.BlockSpec((qw, qw), const),
            pl.BlockSpec((HEAD_DIM // 2, tm), lambda b, s: (0, s)),
            pl.BlockSpec((HEAD_DIM // 2, tm), lambda b, s: (0, s)),
            pl.BlockSpec((tm, 128), lambda b, s: (s, 0)),
            pl.BlockSpec((tm, 128), lambda b, s: (s, 0)),
            pl.BlockSpec((HEAD_DIM, 1), const),
            pl.BlockSpec((1, 128), const),
            pl.BlockSpec((1, 128), const),
            pl.BlockSpec((HEAD_DIM, 1), const),
            pl.BlockSpec((pw, pw), const),
            pl.BlockSpec((1, pw), const),
            pl.BlockSpec((1, MEM_HEADS, M, HEAD_DIM), lambda b, s: (b, 0, 0, 0)),
            pl.BlockSpec((1, mw, M), lambda b, s: (b, 0, 0)),
        ],
        out_specs=[
            pl.BlockSpec((1, qw, tm), lambda b, s: (b, 0, s)),
            pl.BlockSpec((1, tm, qw), lambda b, s: (b, s, 0)),
            pl.BlockSpec((1, DIFF_HEADS, 1, vd, tm), lambda b, s: (b, 0, s, 0, 0)),
            pl.BlockSpec((1, tm, pw), lambda b, s: (b, s, 0)),
            pl.BlockSpec((1, tm, mw), lambda b, s: (b, s, 0)),
        ],
        out_shape=[
            jax.ShapeDtypeStruct((B, qw, S), jnp.bfloat16),
            jax.ShapeDtypeStruct((B, S, qw), jnp.bfloat16),
            jax.ShapeDtypeStruct((B, DIFF_HEADS, ns, vd, tm), jnp.bfloat16),
            jax.ShapeDtypeStruct((B, S, pw), jnp.bfloat16),
            jax.ShapeDtypeStruct((B, S, mw), jnp.bfloat16),
        ],
        scratch_shapes=[
            pltpu.VMEM((POOL_HALO, pw), jnp.float32),
            pltpu.VMEM((POOL_HALO + tm, pw), jnp.float32),
        ],
        compiler_params=pltpu.CompilerParams(
            dimension_semantics=("arbitrary", "arbitrary"), vmem_limit_bytes=VMEM_LIMIT),
        name="in_proj",
    )(x, norm1, wt, wr, gmat, cos_t, sin_t, cos_k, sin_k, qg, kg, kgs, mqg, wpool, pscale, mk, mv_t)


def _absmax(ref):
    return lax.fori_loop(0, ref.shape[0], lambda i, m: jnp.maximum(m, jnp.abs(ref[i])), jnp.float32(0.0))


def _diff_attn_kernel(qn_ref, kn_ref, q_t_ref, k_ref, v_t_ref, lq1_ref, lk1_ref, lq2_ref, lk2_ref, subg_ref, o_ref,
                      rhs_ref, m_ref, l_ref, acc_ref, sa_ref, sb_ref):
    qi = pl.program_id(2)
    tq = q_t_ref.shape[2]
    tk = tq
    q_t = q_t_ref[0]
    row = lax.broadcasted_iota(jnp.int32, q_t.shape, 0)
    zero = jnp.zeros_like(q_t)
    rhs_ref[0] = jnp.where(row < HEAD_DIM, q_t, zero)
    rhs_ref[1] = jnp.where(row >= HEAD_DIM, q_t, zero)
    l_ref[...] = jnp.zeros(l_ref.shape, jnp.float32)
    acc_ref[...] = jnp.zeros(acc_ref.shape, jnp.float32)

    bound = (HEAD_DIM ** 0.5) * LOG2E * _absmax(qn_ref) * _absmax(kn_ref)

    def blocks(j):
        k_blk = k_ref[0, pl.ds(pl.multiple_of(j * tk, tk), tk), :]
        v_blk = v_t_ref[0, 0, j]
        return k_blk, v_blk

    def chunk_mask():
        kc = lax.broadcasted_iota(jnp.int32, (tk, tq), 0) // CHUNK
        qc = lax.broadcasted_iota(jnp.int32, (tk, tq), 1) // CHUNK
        return kc <= qc

    def sweep(step):
        lax.fori_loop(0, qi, lambda j, c: (step(j, False), c)[1], 0)
        step(qi, True)

    @pl.when(bound <= FIXED_OFFSET_LIMIT)
    def _():
        def produce(j, s_ref):
            k_blk = k_ref[0, pl.ds(pl.multiple_of(j * tk, tk), tk), :]
            for c in range(2):
                s_ref[c] = _dot(k_blk, rhs_ref[c])

        def consume(j, s_ref, masked):
            v_blk = v_t_ref[0, 0, j]
            mask = chunk_mask() if masked else None
            for c in range(2):
                p = jnp.exp2(s_ref[c] - bound)
                if masked:
                    p = jnp.where(mask, p, 0.0)
                l_ref[c] += jnp.sum(p.reshape(tk // 8, 8, tq), axis=0)
                acc_ref[c] += _dot(v_blk, p.astype(jnp.bfloat16))

        produce(0, sa_ref)

        def pair(i, carry):
            produce(2 * i + 1, sb_ref)
            consume(2 * i, sa_ref, False)
            produce(2 * i + 2, sa_ref)
            consume(2 * i + 1, sb_ref, False)
            return carry

        lax.fori_loop(0, qi // 2, pair, 0)

        @pl.when(qi % 2 == 0)
        def _():
            consume(qi, sa_ref, True)

        @pl.when(qi % 2 == 1)
        def _():
            produce(qi, sb_ref)
            consume(qi - 1, sa_ref, False)
            consume(qi, sb_ref, True)

    @pl.when(bound > FIXED_OFFSET_LIMIT)
    def _():
        m_ref[...] = jnp.full(m_ref.shape, NEG_INF, jnp.float32)

        def step(j, masked):
            k_blk, v_blk = blocks(j)
            mask = chunk_mask() if masked else None
            for c in range(2):
                s = _dot(k_blk, rhs_ref[c])
                if masked:
                    s = jnp.where(mask, s, NEG_INF)
                m_prev = m_ref[c]
                m_new = jnp.maximum(m_prev, jnp.max(s, axis=0, keepdims=True))
                alpha = jnp.exp2(m_prev - m_new)
                p = jnp.exp2(s - m_new)
                l_ref[c, 0:1] = alpha * l_ref[c, 0:1] + jnp.sum(p, axis=0, keepdims=True)
                acc_ref[c] = alpha * acc_ref[c] + _dot(v_blk, p.astype(jnp.bfloat16))
                m_ref[c] = m_new
        sweep(step)

    lam = (jnp.exp(jnp.sum(lq1_ref[...] * lk1_ref[...], axis=-1, keepdims=True))
           - jnp.exp(jnp.sum(lq2_ref[...] * lk2_ref[...], axis=-1, keepdims=True)) + LAM_INIT)
    l0 = jnp.sum(l_ref[0], axis=0, keepdims=True)
    l1 = jnp.sum(l_ref[1], axis=0, keepdims=True)
    o = acc_ref[0] / l0 - lam * (acc_ref[1] / l1)
    y = _rms_cols(o, subg_ref[...]) * (1.0 - LAM_INIT)
    o_ref[0] = y.T.astype(jnp.bfloat16)


def _diff_attn(q_norm, k_norm, q_t, k, v_t, lq1, lk1, lq2, lk2, subg_col):
    B, qw, S = q_t.shape
    tq = TOKEN_TILE
    nq = S // tq
    vd = 2 * HEAD_DIM
    lam_spec = pl.BlockSpec((1, HEAD_DIM), lambda b, h, i: (0, 0))
    smem = pl.BlockSpec(memory_space=pltpu.SMEM)
    return pl.pallas_call(
        _diff_attn_kernel,
        grid=(B, DIFF_HEADS, nq),
        in_specs=[
            smem, smem,
            pl.BlockSpec((1, vd, tq), lambda b, h, i: (b, h, i)),
            pl.BlockSpec((1, S, vd), lambda b, h, i: (b, 0, h)),
            pl.BlockSpec((1, 1, nq, vd, tq), lambda b, h, i: (b, h, 0, 0, 0)),
            lam_spec, lam_spec, lam_spec, lam_spec,
            pl.BlockSpec((vd, 1), lambda b, h, i: (0, 0)),
        ],
        out_specs=pl.BlockSpec((1, tq, vd), lambda b, h, i: (b, i, h)),
        out_shape=jax.ShapeDtypeStruct((B, S, DIFF_HEADS * vd), jnp.bfloat16),
        scratch_shapes=[
            pltpu.VMEM((2, vd, tq), jnp.bfloat16),
            pltpu.VMEM((2, 1, tq), jnp.float32),
            pltpu.VMEM((2, 8, tq), jnp.float32),
            pltpu.VMEM((2, vd, tq), jnp.float32),
            pltpu.VMEM((2, tq, tq), jnp.float32),
            pltpu.VMEM((2, tq, tq), jnp.float32),
        ],
        compiler_params=pltpu.CompilerParams(
            dimension_semantics=("arbitrary", "arbitrary", "arbitrary"), vmem_limit_bytes=VMEM_LIMIT),
        name="diff_attn",
    )(q_norm, k_norm, q_t, k, v_t, lq1, lk1, lq2, lk2, subg_col)


def _merge_kernel(x_ref, norm1_ref, wg_ref, bg_ref, ya_ref, yb_ref, yc_ref, wa_ref, wb_ref, wc_ref, wo_ref,
                  norm2_ref, wr_hi_ref, wr_lo_ref, br_ref, x1_ref, h2_ref, idx_ref, gw_ref):
    D = x_ref.shape[2]
    x = x_ref[0]
    h = _rms_rows(x, norm1_ref[...]).astype(jnp.bfloat16)
    merged = None
    for i, (y_ref, w_ref) in enumerate(((ya_ref, wa_ref), (yb_ref, wb_ref), (yc_ref, wc_ref))):
        gz = _dot(h, wg_ref[:, i * D:(i + 1) * D]) + bg_ref[:, i * D:(i + 1) * D]
        gate = 1.0 / (1.0 + jnp.exp(-gz))
        term = gate * _dot(y_ref[0], w_ref[...])
        merged = term if merged is None else merged + term
    x1 = x + _dot(merged.astype(jnp.bfloat16), wo_ref[...])
    x1_ref[0] = x1
    h2 = _rms_rows(x1, norm2_ref[...])
    _store_token_tiles(h2_ref, h2)

    hi = h2.astype(jnp.bfloat16)
    lo = (h2 - hi.astype(jnp.float32)).astype(jnp.bfloat16)
    logits = (_dot_nt(wr_hi_ref[...], hi) + _dot_nt(wr_hi_ref[...], lo) + _dot_nt(wr_lo_ref[...], hi)
              + br_ref[...])
    eidx = lax.broadcasted_iota(jnp.int32, logits.shape, 0)
    vals = logits
    top_v, top_i = [], []
    for _ in range(TOP_K):
        mx = jnp.max(vals, axis=0, keepdims=True)
        ix = jnp.min(jnp.where(vals == mx, eidx, N_EXPERTS), axis=0, keepdims=True)
        top_v.append(mx)
        top_i.append(ix)
        vals = jnp.where(eidx == ix, -jnp.inf, vals)
    ex = [jnp.exp(v - top_v[0]) for v in top_v]
    den = ex[0] + ex[1] + ex[2] + ex[3]
    for k in range(TOP_K):
        idx_ref[k, 0] = top_i[k]
        gw_ref[k, 0] = ex[k] / den


def _merge(x, norm1, wg, bg, ya, yb, yc, wa, wb, wc, wo, norm2, wr_hi, wr_lo, br_col):
    B, S, D = x.shape
    tm = TOKEN_TILE
    const = lambda b, s: (0, 0)
    tok = lambda b, s: (b, s, 0)
    full = lambda a: pl.BlockSpec(a.shape, const)
    return pl.pallas_call(
        _merge_kernel,
        grid=(B, S // tm),
        in_specs=[
            pl.BlockSpec((1, tm, D), tok), full(norm1), full(wg), full(bg),
            pl.BlockSpec((1, tm, ya.shape[2]), tok), pl.BlockSpec((1, tm, yb.shape[2]), tok),
            pl.BlockSpec((1, tm, yc.shape[2]), tok),
            full(wa), full(wb), full(wc), full(wo), full(norm2), full(wr_hi), full(wr_lo), full(br_col),
        ],
        out_specs=[
            pl.BlockSpec((1, tm, D), tok),
            pl.BlockSpec((tm * SUBLANES, LANES), lambda b, s: (b * (S // tm) + s, 0)),
            pl.BlockSpec((TOP_K, 1, 1, tm), lambda b, s: (0, b * (S // tm) + s, 0, 0)),
            pl.BlockSpec((TOP_K, 1, 1, tm), lambda b, s: (0, b * (S // tm) + s, 0, 0)),
        ],
        out_shape=[
            jax.ShapeDtypeStruct((B, S, D), jnp.float32),
            jax.ShapeDtypeStruct((B * S * SUBLANES, LANES), jnp.float32),
            jax.ShapeDtypeStruct((TOP_K, B * S // tm, 1, tm), jnp.int32),
            jax.ShapeDtypeStruct((TOP_K, B * S // tm, 1, tm), jnp.float32),
        ],
        compiler_params=pltpu.CompilerParams(
            dimension_semantics=("arbitrary", "arbitrary"), vmem_limit_bytes=VMEM_LIMIT),
        name="merge",
    )(x, norm1, wg, bg, ya, yb, yc, wa, wb, wc, wo, norm2, wr_hi, wr_lo, br_col)


def _experts_kernel(nused_ref, bexp_ref, tok_cur_ref, tok_nxt_ref, dst_ref, h2_hbm, roww_ref, wgu_ref, bgu_ref,
                    wd_ref, bd_ref, y4_hbm, xbuf, ybuf, wgu_bf, wd_bf, gsem, ssem):
    b = pl.program_id(0)
    nb = pl.num_programs(0)
    nused = nused_ref[0]
    rows = xbuf.shape[1] // SUBLANES
    F = wd_ref.shape[1]
    slot = b % 2

    def tile(ref, off):
        return ref.at[pl.ds(pl.multiple_of(off, SUBLANES), SUBLANES)]

    def gather(tok_ref, buf_slot):
        for i in range(rows):
            pltpu.make_async_copy(tile(h2_hbm, tok_ref[0, 0, i]), xbuf.at[buf_slot, pl.ds(i * SUBLANES, SUBLANES)],
                                  gsem.at[buf_slot]).start()

    def wait_rows(buf, sem):
        pltpu.make_async_copy(buf, buf, sem).wait()

    @pl.when(b == 0)
    def _():
        gather(tok_cur_ref, 0)
        ybuf[...] = jnp.zeros_like(ybuf)
        n_real = y4_hbm.shape[0] - 2 * rows * SUBLANES
        for s in range(2):
            spare = pltpu.make_async_copy(
                ybuf.at[s], y4_hbm.at[pl.ds(n_real + s * rows * SUBLANES, rows * SUBLANES)], ssem.at[s])
            spare.start()
            spare.wait()

    @pl.when(jnp.logical_and(b >= 2, b - 2 < nused))
    def _():
        wait_rows(ybuf.at[slot], ssem.at[slot])

    changed = jnp.logical_or(b == 0, bexp_ref[b] != bexp_ref[jnp.maximum(b - 1, 0)])

    @pl.when(jnp.logical_and(changed, b < nused))
    def _():
        chunk = 128
        def cast(i, carry):
            r = pl.ds(pl.multiple_of(i * chunk, chunk), chunk)
            wgu_bf[r, :] = wgu_ref[0, r, :].astype(jnp.bfloat16)
            wd_bf[r, :] = wd_ref[0, r, :].astype(jnp.bfloat16)
            return carry
        lax.fori_loop(0, wgu_bf.shape[0] // chunk, cast, 0)

    @pl.when(jnp.logical_or(b == 0, b <= nused))
    def _():
        wait_rows(xbuf.at[slot], gsem.at[slot])

    @pl.when(b < nused)
    def _():
        gather(tok_nxt_ref, 1 - slot)
        x = _load_token_tiles(xbuf.at[slot], rows).astype(jnp.bfloat16)
        gu = _dot(x, wgu_bf[...]) + bgu_ref[0]
        gate = jnp.minimum(gu[:, :F], SWIGLU_LIMIT)
        up = jnp.clip(gu[:, F:], -SWIGLU_LIMIT, SWIGLU_LIMIT)
        act = gate * (1.0 / (1.0 + jnp.exp(-SWIGLU_ALPHA * gate))) * (up + 1.0)
        y = _dot(act.astype(jnp.bfloat16), wd_bf[...]) + bd_ref[0]
        _store_token_tiles(ybuf.at[slot], y * roww_ref[0])
        for i in range(rows):
            pltpu.make_async_copy(ybuf.at[slot, pl.ds(i * SUBLANES, SUBLANES)], tile(y4_hbm, dst_ref[0, 0, i]),
                                  ssem.at[slot]).start()

    @pl.when(b == nb - 1)
    def _():
        @pl.when(jnp.logical_and(b >= 1, b - 1 < nused))
        def _():
            wait_rows(ybuf.at[1 - slot], ssem.at[1 - slot])

        @pl.when(b < nused)
        def _():
            wait_rows(ybuf.at[slot], ssem.at[slot])
            wait_rows(xbuf.at[1 - slot], gsem.at[1 - slot])


def _experts(nused, blk_exp, row_tok, row_dst, h2, row_w, wgu, bgu, wd, bd, n_out_rows):
    D = wgu.shape[1]
    nb = row_tok.shape[0]
    rows = row_tok.shape[2]
    F = wd.shape[1]
    grid_spec = pltpu.PrefetchScalarGridSpec(
        num_scalar_prefetch=2,
        grid=(nb,),
        in_specs=[
            pl.BlockSpec((1, 1, rows), lambda b, nu, be: (b, 0, 0), memory_space=pltpu.SMEM),
            pl.BlockSpec((1, 1, rows), lambda b, nu, be: (jnp.minimum(b + 1, nb - 1), 0, 0), memory_space=pltpu.SMEM),
            pl.BlockSpec((1, 1, rows), lambda b, nu, be: (b, 0, 0), memory_space=pltpu.SMEM),
            pl.BlockSpec(memory_space=pl.ANY),
            pl.BlockSpec((1, rows, 1), lambda b, nu, be: (b, 0, 0)),
            pl.BlockSpec((1, D, 2 * F), lambda b, nu, be: (be[b], 0, 0)),
            pl.BlockSpec((1, 1, 2 * F), lambda b, nu, be: (be[b], 0, 0)),
            pl.BlockSpec((1, F, D), lambda b, nu, be: (be[b], 0, 0)),
            pl.BlockSpec((1, 1, D), lambda b, nu, be: (be[b], 0, 0)),
        ],
        out_specs=pl.BlockSpec(memory_space=pl.ANY),
        scratch_shapes=[
            pltpu.VMEM((2, rows * SUBLANES, LANES), jnp.float32),
            pltpu.VMEM((2, rows * SUBLANES, LANES), jnp.float32),
            pltpu.VMEM((D, 2 * F), jnp.bfloat16),
            pltpu.VMEM((F, D), jnp.bfloat16),
            pltpu.SemaphoreType.DMA((2,)),
            pltpu.SemaphoreType.DMA((2,)),
        ],
    )
    return pl.pallas_call(
        _experts_kernel,
        grid_spec=grid_spec,
        out_shape=jax.ShapeDtypeStruct((n_out_rows * SUBLANES, LANES), jnp.float32),
        compiler_params=pltpu.CompilerParams(
            dimension_semantics=("arbitrary",), vmem_limit_bytes=VMEM_LIMIT),
        name="experts",
    )(nused, blk_exp, row_tok, row_tok, row_dst, h2, row_w, wgu, bgu, wd, bd)


def _combine_kernel(x1_ref, y0_ref, y1_ref, y2_ref, y3_ref, o_ref):
    tm = x1_ref.shape[0]
    for c in range(SUBLANES):
        rows = pl.ds(c, tm, stride=SUBLANES)
        cols = slice(c * LANES, (c + 1) * LANES)
        o_ref[:, cols] = x1_ref[:, cols] + (((y0_ref[rows, :] + y1_ref[rows, :]) + y2_ref[rows, :]) + y3_ref[rows, :])


def _combine(x1, y4):
    T, D = x1.shape
    tm = TOKEN_TILE
    nt = T // tm
    return pl.pallas_call(
        _combine_kernel,
        grid=(nt,),
        in_specs=[pl.BlockSpec((tm, D), lambda i: (i, 0))]
        + [pl.BlockSpec((tm * SUBLANES, LANES), functools.partial(lambda i, k: (k * nt + i, 0), k=k))
           for k in range(TOP_K)],
        out_specs=pl.BlockSpec((tm, D), lambda i: (i, 0)),
        out_shape=jax.ShapeDtypeStruct((T, D), jnp.float32),
        name="combine",
    )(x1, y4, y4, y4, y4)


def _rope_tables(seq):
    inv_freq = 1.0 / (ROPE_THETA ** (jnp.arange(0, HEAD_DIM, 2, dtype=jnp.float32) / HEAD_DIM))
    ang = jnp.arange(seq, dtype=jnp.float32)[:, None] * inv_freq[None, :]
    return jnp.cos(ang), jnp.sin(ang)


def _dispatch(top_idx, gate_w, rows):
    T = top_idx.shape[1]
    A = T * TOP_K
    e_flat = top_idx.reshape(A)
    order = jnp.argsort(e_flat).astype(jnp.int32)
    experts = jnp.arange(N_EXPERTS, dtype=jnp.int32)
    counts = jnp.sum((e_flat[:, None] == experts[None, :]).astype(jnp.int32), axis=0)
    padded = ((counts + rows - 1) // rows) * rows
    pend = jnp.cumsum(padded)
    pstart = pend - padded
    ustart = jnp.cumsum(counts) - counts
    P = A + N_EXPERTS * rows
    nb = P // rows
    blk_start = jnp.arange(nb, dtype=jnp.int32) * rows
    blk_exp = jnp.minimum(jnp.sum((pend[None, :] <= blk_start[:, None]).astype(jnp.int32), axis=1), N_EXPERTS - 1)
    in_blk = jnp.arange(rows, dtype=jnp.int32)[None, :]
    j = blk_start[:, None] + in_blk - pstart[blk_exp][:, None]
    valid = j < counts[blk_exp][:, None]
    a = order[jnp.clip(ustart[blk_exp][:, None] + j, 0, A - 1)]
    row_tok = jnp.where(valid, a % T, 0)
    row_dst = jnp.where(valid, a, A + (jnp.arange(nb, dtype=jnp.int32)[:, None] % 2) * rows + in_blk)
    row_w = jnp.where(valid, gate_w.reshape(A)[a], 0.0)
    nused = (pend[-1] // rows).astype(jnp.int32).reshape(1)
    return (nused, blk_exp, (row_tok * SUBLANES).reshape(nb, 1, rows), (row_dst * SUBLANES).reshape(nb, 1, rows),
            row_w.reshape(nb, rows, 1), A + 2 * rows)


def kernel(x, mem, norm1, w_in, b_gate, q_norm, k_norm, lambda_q1, lambda_k1, lambda_q2, lambda_k2, diff_subln,
           w_pool, pool_scale, mem_norm, w_mem_kv, mq_norm, mk_norm, w_br_diff, w_br_pool, w_br_mem, w_out, norm2,
           w_router, b_router, w_gate_up, b_gate_up, w_down, b_down):
    B, S, D = x.shape
    depth = norm1.shape[0]
    bf = jnp.bfloat16
    qw = 2 * DIFF_HEADS * HEAD_DIM
    vw = DIFF_HEADS * 2 * HEAD_DIM
    pw = w_pool.shape[1] * w_pool.shape[2]
    mw = MEM_HEADS * HEAD_DIM
    k_off, v_off, pool_off, mq_off, gate_off = qw, 2 * qw, 2 * qw + vw, 2 * qw + vw + pw, 2 * qw + vw + pw + mw
    half = HEAD_DIM // 2

    hpd = (jnp.arange(2)[None, :, None] * (DIFF_HEADS * HEAD_DIM) + jnp.arange(DIFF_HEADS)[:, None, None] * HEAD_DIM
           + jnp.arange(HEAD_DIM)[None, None, :])
    perm = hpd.reshape(-1)
    perm_swapped = (hpd - hpd % HEAD_DIM + (hpd % HEAD_DIM + half) % HEAD_DIM).reshape(-1)
    cos, sin = _rope_tables(S)
    cos_t, sin_t = cos.T, sin.T
    cos_k = jnp.tile(cos, (1, 4))
    sin_k = jnp.tile(jnp.concatenate([-sin, sin], axis=1), (1, 2))
    gidx = jnp.arange(qw) // HEAD_DIM
    gmat = (gidx[:, None] == gidx[None, :]).astype(bf)
    pidx = jnp.arange(pw) // w_pool.shape[2]

    for l in range(depth):
        wi = w_in[l]
        wt = jnp.concatenate([wi[:, perm], wi[:, v_off:pool_off], wi[:, mq_off:gate_off]], axis=1).T.astype(bf)
        wr = jnp.concatenate([wi[:, k_off + perm], wi[:, k_off + perm_swapped], wi[:, pool_off:mq_off]],
                             axis=1).astype(bf)
        wg = wi[:, gate_off:].astype(bf)
        kg = jnp.tile(k_norm[l], 2)[None, :]
        kgs = jnp.tile(jnp.roll(k_norm[l], half), 2)[None, :]
        wpool_bd = jnp.where(pidx[:, None] == pidx[None, :],
                             jnp.tile(w_pool[l].reshape(pw, -1), (1, w_pool.shape[1])), 0.0).astype(bf)
        mk, mv_t = _mem_kv(mem, mem_norm[l][None, :], w_mem_kv[l].T.astype(bf), mk_norm[l][:, None])
        q_t, k, v_t, y_b, y_c = _in_proj(
            x, norm1[l][None, :], wt, wr, gmat, cos_t, sin_t, cos_k, sin_k, q_norm[l][:, None], kg, kgs,
            mq_norm[l][:, None], wpool_bd, pool_scale[l][None, :], mk, mv_t)
        y_a = _diff_attn(q_norm[l], k_norm[l], q_t, k, v_t, lambda_q1[l][None, :], lambda_k1[l][None, :], lambda_q2[l][None, :],
                         lambda_k2[l][None, :], diff_subln[l][:, None])
        wr_t = w_router[l].T
        wr_hi = wr_t.astype(bf)
        wr_lo = (wr_t - wr_hi.astype(jnp.float32)).astype(bf)
        x1, h2, top_idx, gate_w = _merge(
            x, norm1[l][None, :], wg, b_gate[l][None, :], y_a, y_b, y_c, w_br_diff[l].astype(bf),
            w_br_pool[l].astype(bf), w_br_mem[l].astype(bf), w_out[l].astype(bf), norm2[l][None, :], wr_hi, wr_lo,
            b_router[l][:, None])
        T = B * S
        top_idx = top_idx.reshape(TOP_K, T)
        gate_w = gate_w.reshape(TOP_K, T)
        nused, blk_exp, row_tok, row_dst, row_w, P = _dispatch(top_idx, gate_w, EXPERT_ROWS)
        y4 = _experts(nused, blk_exp, row_tok, row_dst, h2, row_w, w_gate_up[l],
                      b_gate_up[l][:, None, :], w_down[l], b_down[l][:, None, :], P)
        x = _combine(x1.reshape(T, D), y4).reshape(B, S, D)
    return x
```

```python
import functools
import math

import jax
import jax.numpy as jnp
from jax import lax
from jax.experimental import pallas as pl
from jax.experimental.pallas import tpu as pltpu

HEAD_DIM = 64
CHUNK = 64
RMS_EPS = 1e-6
ROPE_THETA = 10000.0
DIFF_HEADS = 4
MEM_HEADS = 4
POOL_WINDOWS = (2, 4, 8, 16)
POOL_HALO = 16
N_EXPERTS = 32
TOP_K = 4
SWIGLU_LIMIT = 7.0
SWIGLU_ALPHA = 1.702
NEG_INF = -1e30
LAM_INIT = 0.8 - 0.6 * math.exp(-0.3 * 0)
LOG2E = math.log2(math.e)
FIXED_OFFSET_LIMIT = 40.0 * LOG2E

SUBLANES, LANES = 8, 128
TOKEN_TILE = 512
EXPERT_ROWS = 256
VMEM_LIMIT = 56 * 1024 * 1024

_NT = (((1,), (1,)), ((), ()))


def _dot(a, b):
    return jnp.dot(a, b, preferred_element_type=jnp.float32)


def _dot_nt(a, b):
    return lax.dot_general(a, b, _NT, preferred_element_type=jnp.float32)


def _store_token_tiles(ref, x):
    n = x.shape[0]
    for c in range(SUBLANES):
        ref[pl.ds(c, n, stride=SUBLANES), :] = x[:, c * LANES:(c + 1) * LANES]


def _load_token_tiles(ref, n):
    return jnp.concatenate([ref[pl.ds(c, n, stride=SUBLANES), :] for c in range(SUBLANES)], axis=1)


def _rms_rows(x, gain_row):
    return x * lax.rsqrt(jnp.mean(x * x, axis=-1, keepdims=True) + RMS_EPS) * gain_row


def _rms_cols(x, gain_col):
    return x * lax.rsqrt(jnp.mean(x * x, axis=0, keepdims=True) + RMS_EPS) * gain_col


def _mem_kv_kernel(mem_ref, gain_ref, wkv_t_ref, mk_gain_ref, mk_ref, mv_t_ref):
    mem_n = _rms_rows(mem_ref[0], gain_ref[...]).astype(jnp.bfloat16)
    kv_t = _dot_nt(wkv_t_ref[...], mem_n)
    mw = kv_t.shape[0] // 2
    for h in range(MEM_HEADS):
        blk = kv_t[h * HEAD_DIM:(h + 1) * HEAD_DIM]
        mk_t = _rms_cols(blk, mk_gain_ref[...]) * (HEAD_DIM ** -0.5)
        mk_ref[0, h] = mk_t.T.astype(jnp.bfloat16)
    mv_t_ref[0] = kv_t[mw:].astype(jnp.bfloat16)


def _mem_kv(mem, mem_norm, wkv_t, mk_gain_col):
    B, M, D = mem.shape
    mw = wkv_t.shape[0] // 2
    return pl.pallas_call(
        _mem_kv_kernel,
        grid=(B,),
        in_specs=[
            pl.BlockSpec((1, M, D), lambda b: (b, 0, 0)),
            pl.BlockSpec((1, D), lambda b: (0, 0)),
            pl.BlockSpec((2 * mw, D), lambda b: (0, 0)),
            pl.BlockSpec((HEAD_DIM, 1), lambda b: (0, 0)),
        ],
        out_specs=[
            pl.BlockSpec((1, MEM_HEADS, M, HEAD_DIM), lambda b: (b, 0, 0, 0)),
            pl.BlockSpec((1, mw, M), lambda b: (b, 0, 0)),
        ],
        out_shape=[
            jax.ShapeDtypeStruct((B, MEM_HEADS, M, HEAD_DIM), jnp.bfloat16),
            jax.ShapeDtypeStruct((B, mw, M), jnp.bfloat16),
        ],
        name="mem_kv",
    )(mem, mem_norm, wkv_t, mk_gain_col)


def _in_proj_kernel(x_ref, norm1_ref, wt_ref, wr_ref, gmat_ref, cos_t_ref, sin_t_ref, cos_k_ref, sin_k_ref,
                    qg_ref, kg_ref, kgs_ref, mqg_ref, wpool_ref, pscale_ref, mk_ref, mv_t_ref,
                    q_t_ref, k_ref, v_t_ref, yb_ref, yc_ref, halo_ref, ext_ref):
    s_idx = pl.program_id(1)
    tm = x_ref.shape[1]
    h = _rms_rows(x_ref[0], norm1_ref[...]).astype(jnp.bfloat16)
    z_t = _dot_nt(wt_ref[...], h)
    z_r = _dot(h, wr_ref[...])

    cos_t, sin_t = cos_t_ref[...], sin_t_ref[...]
    half = HEAD_DIM // 2
    for g in range(2 * DIFF_HEADS):
        y = _rms_cols(z_t[g * HEAD_DIM:(g + 1) * HEAD_DIM], qg_ref[...]) * (HEAD_DIM ** -0.5 * LOG2E)
        t1, t2 = y[:half], y[half:]
        q_t_ref[0, g * HEAD_DIM:g * HEAD_DIM + half] = (t1 * cos_t - t2 * sin_t).astype(jnp.bfloat16)
        q_t_ref[0, g * HEAD_DIM + half:(g + 1) * HEAD_DIM] = (t2 * cos_t + t1 * sin_t).astype(jnp.bfloat16)

    qw = 2 * DIFF_HEADS * HEAD_DIM
    vd = 2 * HEAD_DIM
    for hd in range(DIFF_HEADS):
        v_t_ref[0, hd, 0] = z_t[qw + hd * vd:qw + (hd + 1) * vd].astype(jnp.bfloat16)

    mq_off = qw + DIFF_HEADS * vd
    outs = []
    for hd in range(MEM_HEADS):
        mq_t = _rms_cols(z_t[mq_off + hd * HEAD_DIM:mq_off + (hd + 1) * HEAD_DIM], mqg_ref[...])
        s_t = _dot(mk_ref[0, hd], mq_t.astype(jnp.bfloat16))
        p_t = jnp.exp(s_t - jnp.max(s_t, axis=0, keepdims=True))
        o_t = _dot(mv_t_ref[0, hd * HEAD_DIM:(hd + 1) * HEAD_DIM], p_t.astype(jnp.bfloat16))
        outs.append(o_t / jnp.sum(p_t, axis=0, keepdims=True))
    yc_ref[0] = jnp.concatenate(outs, axis=0).T.astype(jnp.bfloat16)

    kw = 2 * DIFF_HEADS * HEAD_DIM
    zk, zks = z_r[:, :kw], z_r[:, kw:2 * kw]
    ssq = _dot((zk * zk).astype(jnp.bfloat16), gmat_ref[...])
    r = lax.rsqrt(ssq * (1.0 / HEAD_DIM) + RMS_EPS)
    cos_k, sin_k = cos_k_ref[...], sin_k_ref[...]
    for j in range(kw // 128):
        sl = slice(j * 128, (j + 1) * 128)
        kk = r[:, sl] * (zk[:, sl] * kg_ref[...] * cos_k + zks[:, sl] * kgs_ref[...] * sin_k)
        k_ref[0, :, sl] = kk.astype(jnp.bfloat16)

    u = z_r[:, 2 * kw:]

    @pl.when(s_idx == 0)
    def _():
        halo_ref[...] = jnp.zeros_like(halo_ref)

    ext_ref[0:POOL_HALO] = halo_ref[...]
    ext_ref[POOL_HALO:] = u
    halo_ref[...] = u[tm - POOL_HALO:]
    pos1 = (s_idx * tm + 1 + lax.broadcasted_iota(jnp.int32, (tm, 128), 0)).astype(jnp.float32)
    lane = lax.broadcasted_iota(jnp.int32, (tm, 128), 1)
    pooled = []
    for part in range(2):
        cols = slice(part * 128, (part + 1) * 128)
        w_a, w_b = POOL_WINDOWS[2 * part], POOL_WINDOWS[2 * part + 1]
        acc = ext_ref[POOL_HALO:POOL_HALO + tm, cols]
        sum_a = None
        for j in range(1, w_b):
            if j == w_a:
                sum_a = acc
            acc = acc + ext_ref[POOL_HALO - j:POOL_HALO - j + tm, cols]
        mean_a = sum_a / jnp.minimum(pos1, float(w_a))
        mean_b = acc / jnp.minimum(pos1, float(w_b))
        pooled.append(jnp.where(lane < 64, mean_a, mean_b) - u[:, cols])
    pooled = jnp.concatenate(pooled, axis=1).astype(jnp.bfloat16)
    yb_ref[0] = (_dot(pooled, wpool_ref[...]) * pscale_ref[...]).astype(jnp.bfloat16)


def _in_proj(x, norm1, wt, wr, gmat, cos_t, sin_t, cos_k, sin_k, qg, kg, kgs, mqg, wpool, pscale, mk, mv_t):
    B, S, D = x.shape
    tm = TOKEN_TILE
    ns = S // tm
    nt, nr = wt.shape[0], wr.shape[1]
    qw = 2 * DIFF_HEADS * HEAD_DIM
    vd = 2 * HEAD_DIM
    pw = wpool.shape[0]
    M = mk.shape[2]
    mw = mv_t.shape[1]
    const = lambda b, s: (0, 0)
    return pl.pallas_call(
        _in_proj_kernel,
        grid=(B, ns),
        in_specs=[
            pl.BlockSpec((1, tm, D), lambda b, s: (b, s, 0)),
            pl.BlockSpec((1, D), const),
            pl.BlockSpec((nt, D), const),
            pl.BlockSpec((D, nr), const),
            pl.BlockSpec((qw, qw), const),
            pl.BlockSpec((HEAD_DIM // 2, tm), lambda b, s: (0, s)),
            pl.BlockSpec((HEAD_DIM // 2, tm), lambda b, s: (0, s)),
            pl.BlockSpec((tm, 128), lambda b, s: (s, 0)),
            pl.BlockSpec((tm, 128), lambda b, s: (s, 0)),
            pl.BlockSpec((HEAD_DIM, 1), const),
            pl.BlockSpec((1, 128), const),
            pl.BlockSpec((1, 128), const),
            pl.BlockSpec((HEAD_DIM, 1), const),
            pl.BlockSpec((pw, pw), const),
            pl.BlockSpec((1, pw), const),
            pl.BlockSpec((1, MEM_HEADS, M, HEAD_DIM), lambda b, s: (b, 0, 0, 0)),
            pl.BlockSpec((1, mw, M), lambda b, s: (b, 0, 0)),
        ],
        out_specs=[
            pl.BlockSpec((1, qw, tm), lambda b, s: (b, 0, s)),
            pl.BlockSpec((1, tm, qw), lambda b, s: (b, s, 0)),
            pl.BlockSpec((1, DIFF_HEADS, 1, vd, tm), lambda b, s: (b, 0, s, 0, 0)),
            pl.BlockSpec((1, tm, pw), lambda b, s: (b, s, 0)),
            pl.BlockSpec((1, tm, mw), lambda b, s: (b, s, 0)),
        ],
        out_shape=[
            jax.ShapeDtypeStruct((B, qw, S), jnp.bfloat16),
            jax.ShapeDtypeStruct((B, S, qw), jnp.bfloat16),
            jax.ShapeDtypeStruct((B, DIFF_HEADS, ns, vd, tm), jnp.bfloat16),
            jax.ShapeDtypeStruct((B, S, pw), jnp.bfloat16),
            jax.ShapeDtypeStruct((B, S, mw), jnp.bfloat16),
        ],
        scratch_shapes=[
            pltpu.VMEM((POOL_HALO, pw), jnp.float32),
            pltpu.VMEM((POOL_HALO + tm, pw), jnp.float32),
        ],
        compiler_params=pltpu.CompilerParams(
            dimension_semantics=("arbitrary", "arbitrary"), vmem_limit_bytes=VMEM_LIMIT),
        name="in_proj",
    )(x, norm1, wt, wr, gmat, cos_t, sin_t, cos_k, sin_k, qg, kg, kgs, mqg, wpool, pscale, mk, mv_t)


def _absmax(ref):
    return lax.fori_loop(0, ref.shape[0], lambda i, m: jnp.maximum(m, jnp.abs(ref[i])), jnp.float32(0.0))


def _diff_attn_kernel(qn_ref, kn_ref, q_t_ref, k_ref, v_t_ref, lq1_ref, lk1_ref, lq2_ref, lk2_ref, subg_ref, o_ref,
                      rhs_ref, m_ref, l_ref, acc_ref, sa_ref, sb_ref):
    qi = pl.program_id(2)
    tq = q_t_ref.shape[2]
    tk = tq
    q_t = q_t_ref[0]
    row = lax.broadcasted_iota(jnp.int32, q_t.shape, 0)
    zero = jnp.zeros_like(q_t)
    rhs_ref[0] = jnp.where(row < HEAD_DIM, q_t, zero)
    rhs_ref[1] = jnp.where(row >= HEAD_DIM, q_t, zero)
    l_ref[...] = jnp.zeros(l_ref.shape, jnp.float32)
    acc_ref[...] = jnp.zeros(acc_ref.shape, jnp.float32)

    bound = (HEAD_DIM ** 0.5) * LOG2E * _absmax(qn_ref) * _absmax(kn_ref)

    def blocks(j):
        k_blk = k_ref[0, pl.ds(pl.multiple_of(j * tk, tk), tk), :]
        v_blk = v_t_ref[0, 0, j]
        return k_blk, v_blk

    def chunk_mask():
        kc = lax.broadcasted_iota(jnp.int32, (tk, tq), 0) // CHUNK
        qc = lax.broadcasted_iota(jnp.int32, (tk, tq), 1) // CHUNK
        return kc <= qc

    def sweep(step):
        lax.fori_loop(0, qi, lambda j, c: (step(j, False), c)[1], 0)
        step(qi, True)

    @pl.when(bound <= FIXED_OFFSET_LIMIT)
    def _():
        def produce(j, s_ref):
            k_blk = k_ref[0, pl.ds(pl.multiple_of(j * tk, tk), tk), :]
            for c in range(2):
                s_ref[c] = _dot(k_blk, rhs_ref[c])

        def consume(j, s_ref, masked):
            v_blk = v_t_ref[0, 0, j]
            mask = chunk_mask() if masked else None
            for c in range(2):
                p = jnp.exp2(s_ref[c] - bound)
                if masked:
                    p = jnp.where(mask, p, 0.0)
                l_ref[c] += jnp.sum(p.reshape(tk // 8, 8, tq), axis=0)
                acc_ref[c] += _dot(v_blk, p.astype(jnp.bfloat16))

        produce(0, sa_ref)

        def pair(i, carry):
            produce(2 * i + 1, sb_ref)
            consume(2 * i, sa_ref, False)
            produce(2 * i + 2, sa_ref)
            consume(2 * i + 1, sb_ref, False)
            return carry

        lax.fori_loop(0, qi // 2, pair, 0)

        @pl.when(qi % 2 == 0)
        def _():
            consume(qi, sa_ref, True)

        @pl.when(qi % 2 == 1)
        def _():
            produce(qi, sb_ref)
            consume(qi - 1, sa_ref, False)
            consume(qi, sb_ref, True)

    @pl.when(bound > FIXED_OFFSET_LIMIT)
    def _():
        m_ref[...] = jnp.full(m_ref.shape, NEG_INF, jnp.float32)

        def step(j, masked):
            k_blk, v_blk = blocks(j)
            mask = chunk_mask() if masked else None
            for c in range(2):
                s = _dot(k_blk, rhs_ref[c])
                if masked:
                    s = jnp.where(mask, s, NEG_INF)
                m_prev = m_ref[c]
                m_new = jnp.maximum(m_prev, jnp.max(s, axis=0, keepdims=True))
                alpha = jnp.exp2(m_prev - m_new)
                p = jnp.exp2(s - m_new)
                l_ref[c, 0:1] = alpha * l_ref[c, 0:1] + jnp.sum(p, axis=0, keepdims=True)
                acc_ref[c] = alpha * acc_ref[c] + _dot(v_blk, p.astype(jnp.bfloat16))
                m_ref[c] = m_new
        sweep(step)

    lam = (jnp.exp(jnp.sum(lq1_ref[...] * lk1_ref[...], axis=-1, keepdims=True))
           - jnp.exp(jnp.sum(lq2_ref[...] * lk2_ref[...], axis=-1, keepdims=True)) + LAM_INIT)
    l0 = jnp.sum(l_ref[0], axis=0, keepdims=True)
    l1 = jnp.sum(l_ref[1], axis=0, keepdims=True)
    o = acc_ref[0] / l0 - lam * (acc_ref[1] / l1)
    y = _rms_cols(o, subg_ref[...]) * (1.0 - LAM_INIT)
    o_ref[0] = y.T.astype(jnp.bfloat16)


def _diff_attn(q_norm, k_norm, q_t, k, v_t, lq1, lk1, lq2, lk2, subg_col):
    B, qw, S = q_t.shape
    tq = TOKEN_TILE
    nq = S // tq
    vd = 2 * HEAD_DIM
    lam_spec = pl.BlockSpec((1, HEAD_DIM), lambda b, h, i: (0, 0))
    smem = pl.BlockSpec(memory_space=pltpu.SMEM)
    return pl.pallas_call(
        _diff_attn_kernel,
        grid=(B, DIFF_HEADS, nq),
        in_specs=[
            smem, smem,
            pl.BlockSpec((1, vd, tq), lambda b, h, i: (b, h, i)),
            pl.BlockSpec((1, S, vd), lambda b, h, i: (b, 0, h)),
            pl.BlockSpec((1, 1, nq, vd, tq), lambda b, h, i: (b, h, 0, 0, 0)),
            lam_spec, lam_spec, lam_spec, lam_spec,
            pl.BlockSpec((vd, 1), lambda b, h, i: (0, 0)),
        ],
        out_specs=pl.BlockSpec((1, tq, vd), lambda b, h, i: (b, i, h)),
        out_shape=jax.ShapeDtypeStruct((B, S, DIFF_HEADS * vd), jnp.bfloat16),
        scratch_shapes=[
            pltpu.VMEM((2, vd, tq), jnp.bfloat16),
            pltpu.VMEM((2, 1, tq), jnp.float32),
            pltpu.VMEM((2, 8, tq), jnp.float32),
            pltpu.VMEM((2, vd, tq), jnp.float32),
            pltpu.VMEM((2, tq, tq), jnp.float32),
            pltpu.VMEM((2, tq, tq), jnp.float32),
        ],
        compiler_params=pltpu.CompilerParams(
            dimension_semantics=("arbitrary", "arbitrary", "arbitrary"), vmem_limit_bytes=VMEM_LIMIT),
        name="diff_attn",
    )(q_norm, k_norm, q_t, k, v_t, lq1, lk1, lq2, lk2, subg_col)


def _merge_kernel(x_ref, norm1_ref, wg_ref, bg_ref, ya_ref, yb_ref, yc_ref, wa_ref, wb_ref, wc_ref, wo_ref,
                  norm2_ref, wr_hi_ref, wr_lo_ref, br_ref, x1_ref, h2_ref, idx_ref, gw_ref):
    D = x_ref.shape[2]
    x = x_ref[0]
    h = _rms_rows(x, norm1_ref[...]).astype(jnp.bfloat16)
    merged = None
    for i, (y_ref, w_ref) in enumerate(((ya_ref, wa_ref), (yb_ref, wb_ref), (yc_ref, wc_ref))):
        gz = _dot(h, wg_ref[:, i * D:(i + 1) * D]) + bg_ref[:, i * D:(i + 1) * D]
        gate = 1.0 / (1.0 + jnp.exp(-gz))
        term = gate * _dot(y_ref[0], w_ref[...])
        merged = term if merged is None else merged + term
    x1 = x + _dot(merged.astype(jnp.bfloat16), wo_ref[...])
    x1_ref[0] = x1
    h2 = _rms_rows(x1, norm2_ref[...])
    _store_token_tiles(h2_ref, h2)

    hi = h2.astype(jnp.bfloat16)
    lo = (h2 - hi.astype(jnp.float32)).astype(jnp.bfloat16)
    logits = (_dot_nt(wr_hi_ref[...], hi) + _dot_nt(wr_hi_ref[...], lo) + _dot_nt(wr_lo_ref[...], hi)
              + br_ref[...])
    eidx = lax.broadcasted_iota(jnp.int32, logits.shape, 0)
    vals = logits
    top_v, top_i = [], []
    for _ in range(TOP_K):
        mx = jnp.max(vals, axis=0, keepdims=True)
        ix = jnp.min(jnp.where(vals == mx, eidx, N_EXPERTS), axis=0, keepdims=True)
        top_v.append(mx)
        top_i.append(ix)
        vals = jnp.where(eidx == ix, -jnp.inf, vals)
    ex = [jnp.exp(v - top_v[0]) for v in top_v]
    den = ex[0] + ex[1] + ex[2] + ex[3]
    for k in range(TOP_K):
        idx_ref[k, 0] = top_i[k]
        gw_ref[k, 0] = ex[k] / den


def _merge(x, norm1, wg, bg, ya, yb, yc, wa, wb, wc, wo, norm2, wr_hi, wr_lo, br_col):
    B, S, D = x.shape
    tm = TOKEN_TILE
    const = lambda b, s: (0, 0)
    tok = lambda b, s: (b, s, 0)
    full = lambda a: pl.BlockSpec(a.shape, const)
    return pl.pallas_call(
        _merge_kernel,
        grid=(B, S // tm),
        in_specs=[
            pl.BlockSpec((1, tm, D), tok), full(norm1), full(wg), full(bg),
            pl.BlockSpec((1, tm, ya.shape[2]), tok), pl.BlockSpec((1, tm, yb.shape[2]), tok),
            pl.BlockSpec((1, tm, yc.shape[2]), tok),
            full(wa), full(wb), full(wc), full(wo), full(norm2), full(wr_hi), full(wr_lo), full(br_col),
        ],
        out_specs=[
            pl.BlockSpec((1, tm, D), tok),
            pl.BlockSpec((tm * SUBLANES, LANES), lambda b, s: (b * (S // tm) + s, 0)),
            pl.BlockSpec((TOP_K, 1, 1, tm), lambda b, s: (0, b * (S // tm) + s, 0, 0)),
            pl.BlockSpec((TOP_K, 1, 1, tm), lambda b, s: (0, b * (S // tm) + s, 0, 0)),
        ],
        out_shape=[
            jax.ShapeDtypeStruct((B, S, D), jnp.float32),
            jax.ShapeDtypeStruct((B * S * SUBLANES, LANES), jnp.float32),
            jax.ShapeDtypeStruct((TOP_K, B * S // tm, 1, tm), jnp.int32),
            jax.ShapeDtypeStruct((TOP_K, B * S // tm, 1, tm), jnp.float32),
        ],
        compiler_params=pltpu.CompilerParams(
            dimension_semantics=("arbitrary", "arbitrary"), vmem_limit_bytes=VMEM_LIMIT),
        name="merge",
    )(x, norm1, wg, bg, ya, yb, yc, wa, wb, wc, wo, norm2, wr_hi, wr_lo, br_col)


def _experts_kernel(nused_ref, bexp_ref, tok_cur_ref, tok_nxt_ref, dst_ref, h2_hbm, roww_ref, wgu_ref, bgu_ref,
                    wd_ref, bd_ref, y4_hbm, xbuf, ybuf, wgu_bf, wd_bf, gsem, ssem):
    b = pl.program_id(0)
    nb = pl.num_programs(0)
    nused = nused_ref[0]
    rows = xbuf.shape[1] // SUBLANES
    F = wd_ref.shape[1]
    slot = b % 2

    def tile(ref, off):
        return ref.at[pl.ds(pl.multiple_of(off, SUBLANES), SUBLANES)]

    def gather(tok_ref, buf_slot):
        for i in range(rows):
            pltpu.make_async_copy(tile(h2_hbm, tok_ref[0, 0, i]), xbuf.at[buf_slot, pl.ds(i * SUBLANES, SUBLANES)],
                                  gsem.at[buf_slot]).start()

    def wait_rows(buf, sem):
        pltpu.make_async_copy(buf, buf, sem).wait()

    @pl.when(b == 0)
    def _():
        gather(tok_cur_ref, 0)
        ybuf[...] = jnp.zeros_like(ybuf)
        n_real = y4_hbm.shape[0] - 2 * rows * SUBLANES
        for s in range(2):
            spare = pltpu.make_async_copy(
                ybuf.at[s], y4_hbm.at[pl.ds(n_real + s * rows * SUBLANES, rows * SUBLANES)], ssem.at[s])
            spare.start()
            spare.wait()

    @pl.when(jnp.logical_and(b >= 2, b - 2 < nused))
    def _():
        wait_rows(ybuf.at[slot], ssem.at[slot])

    changed = jnp.logical_or(b == 0, bexp_ref[b] != bexp_ref[jnp.maximum(b - 1, 0)])

    @pl.when(jnp.logical_and(changed, b < nused))
    def _():
        chunk = 128
        def cast(i, carry):
            r = pl.ds(pl.multiple_of(i * chunk, chunk), chunk)
            wgu_bf[r, :] = wgu_ref[0, r, :].astype(jnp.bfloat16)
            wd_bf[r, :] = wd_ref[0, r, :].astype(jnp.bfloat16)
            return carry
        lax.fori_loop(0, wgu_bf.shape[0] // chunk, cast, 0)

    @pl.when(jnp.logical_or(b == 0, b <= nused))
    def _():
        wait_rows(xbuf.at[slot], gsem.at[slot])

    @pl.when(b < nused)
    def _():
        gather(tok_nxt_ref, 1 - slot)
        x = _load_token_tiles(xbuf.at[slot], rows).astype(jnp.bfloat16)
        gu = _dot(x, wgu_bf[...]) + bgu_ref[0]
        gate = jnp.minimum(gu[:, :F], SWIGLU_LIMIT)
        up = jnp.clip(gu[:, F:], -SWIGLU_LIMIT, SWIGLU_LIMIT)
        act = gate * (1.0 / (1.0 + jnp.exp(-SWIGLU_ALPHA * gate))) * (up + 1.0)
        y = _dot(act.astype(jnp.bfloat16), wd_bf[...]) + bd_ref[0]
        _store_token_tiles(ybuf.at[slot], y * roww_ref[0])
        for i in range(rows):
            pltpu.make_async_copy(ybuf.at[slot, pl.ds(i * SUBLANES, SUBLANES)], tile(y4_hbm, dst_ref[0, 0, i]),
                                  ssem.at[slot]).start()

    @pl.when(b == nb - 1)
    def _():
        @pl.when(jnp.logical_and(b >= 1, b - 1 < nused))
        def _():
            wait_rows(ybuf.at[1 - slot], ssem.at[1 - slot])

        @pl.when(b < nused)
        def _():
            wait_rows(ybuf.at[slot], ssem.at[slot])
            wait_rows(xbuf.at[1 - slot], gsem.at[1 - slot])


def _experts(nused, blk_exp, row_tok, row_dst, h2, row_w, wgu, bgu, wd, bd, n_out_rows):
    D = wgu.shape[1]
    nb = row_tok.shape[0]
    rows = row_tok.shape[2]
    F = wd.shape[1]
    grid_spec = pltpu.PrefetchScalarGridSpec(
        num_scalar_prefetch=2,
        grid=(nb,),
        in_specs=[
            pl.BlockSpec((1, 1, rows), lambda b, nu, be: (b, 0, 0), memory_space=pltpu.SMEM),
            pl.BlockSpec((1, 1, rows), lambda b, nu, be: (jnp.minimum(b + 1, nb - 1), 0, 0), memory_space=pltpu.SMEM),
            pl.BlockSpec((1, 1, rows), lambda b, nu, be: (b, 0, 0), memory_space=pltpu.SMEM),
            pl.BlockSpec(memory_space=pl.ANY),
            pl.BlockSpec((1, rows, 1), lambda b, nu, be: (b, 0, 0)),
            pl.BlockSpec((1, D, 2 * F), lambda b, nu, be: (be[b], 0, 0)),
            pl.BlockSpec((1, 1, 2 * F), lambda b, nu, be: (be[b], 0, 0)),
            pl.BlockSpec((1, F, D), lambda b, nu, be: (be[b], 0, 0)),
            pl.BlockSpec((1, 1, D), lambda b, nu, be: (be[b], 0, 0)),
        ],
        out_specs=pl.BlockSpec(memory_space=pl.ANY),
        scratch_shapes=[
            pltpu.VMEM((2, rows * SUBLANES, LANES), jnp.float32),
            pltpu.VMEM((2, rows * SUBLANES, LANES), jnp.float32),
            pltpu.VMEM((D, 2 * F), jnp.bfloat16),
            pltpu.VMEM((F, D), jnp.bfloat16),
            pltpu.SemaphoreType.DMA((2,)),
            pltpu.SemaphoreType.DMA((2,)),
        ],
    )
    return pl.pallas_call(
        _experts_kernel,
        grid_spec=grid_spec,
        out_shape=jax.ShapeDtypeStruct((n_out_rows * SUBLANES, LANES), jnp.float32),
        compiler_params=pltpu.CompilerParams(
            dimension_semantics=("arbitrary",), vmem_limit_bytes=VMEM_LIMIT),
        name="experts",
    )(nused, blk_exp, row_tok, row_tok, row_dst, h2, row_w, wgu, bgu, wd, bd)


def _combine_kernel(x1_ref, y0_ref, y1_ref, y2_ref, y3_ref, o_ref):
    tm = x1_ref.shape[0]
    for c in range(SUBLANES):
        rows = pl.ds(c, tm, stride=SUBLANES)
        cols = slice(c * LANES, (c + 1) * LANES)
        o_ref[:, cols] = x1_ref[:, cols] + (((y0_ref[rows, :] + y1_ref[rows, :]) + y2_ref[rows, :]) + y3_ref[rows, :])


def _combine(x1, y4):
    T, D = x1.shape
    tm = TOKEN_TILE
    nt = T // tm
    return pl.pallas_call(
        _combine_kernel,
        grid=(nt,),
        in_specs=[pl.BlockSpec((tm, D), lambda i: (i, 0))]
        + [pl.BlockSpec((tm * SUBLANES, LANES), functools.partial(lambda i, k: (k * nt + i, 0), k=k))
           for k in range(TOP_K)],
        out_specs=pl.BlockSpec((tm, D), lambda i: (i, 0)),
        out_shape=jax.ShapeDtypeStruct((T, D), jnp.float32),
        name="combine",
    )(x1, y4, y4, y4, y4)


def _rope_tables(seq):
    inv_freq = 1.0 / (ROPE_THETA ** (jnp.arange(0, HEAD_DIM, 2, dtype=jnp.float32) / HEAD_DIM))
    ang = jnp.arange(seq, dtype=jnp.float32)[:, None] * inv_freq[None, :]
    return jnp.cos(ang), jnp.sin(ang)


def _dispatch(top_idx, gate_w, rows):
    T = top_idx.shape[1]
    A = T * TOP_K
    e_flat = top_idx.reshape(A)
    order = jnp.argsort(e_flat).astype(jnp.int32)
    experts = jnp.arange(N_EXPERTS, dtype=jnp.int32)
    counts = jnp.sum((e_flat[:, None] == experts[None, :]).astype(jnp.int32), axis=0)
    padded = ((counts + rows - 1) // rows) * rows
    pend = jnp.cumsum(padded)
    pstart = pend - padded
    ustart = jnp.cumsum(counts) - counts
    P = A + N_EXPERTS * rows
    nb = P // rows
    blk_start = jnp.arange(nb, dtype=jnp.int32) * rows
    blk_exp = jnp.minimum(jnp.sum((pend[None, :] <= blk_start[:, None]).astype(jnp.int32), axis=1), N_EXPERTS - 1)
    in_blk = jnp.arange(rows, dtype=jnp.int32)[None, :]
    j = blk_start[:, None] + in_blk - pstart[blk_exp][:, None]
    valid = j < counts[blk_exp][:, None]
    a_flat = jnp.take(order, jnp.clip(ustart[blk_exp][:, None] + j, 0, A - 1).reshape(P))
    w_flat = jnp.take(gate_w.reshape(A), a_flat).reshape(nb, rows)
    a = a_flat.reshape(nb, rows)
    row_tok = jnp.where(valid, a % T, 0)
    row_dst = jnp.where(valid, a, A + (jnp.arange(nb, dtype=jnp.int32)[:, None] % 2) * rows + in_blk)
    row_w = jnp.where(valid, w_flat, 0.0)
    nused = (pend[-1] // rows).astype(jnp.int32).reshape(1)
    return (nused, blk_exp, (row_tok * SUBLANES).reshape(nb, 1, rows), (row_dst * SUBLANES).reshape(nb, 1, rows),
            row_w.reshape(nb, rows, 1), A + 2 * rows)


def kernel(x, mem, norm1, w_in, b_gate, q_norm, k_norm, lambda_q1, lambda_k1, lambda_q2, lambda_k2, diff_subln,
           w_pool, pool_scale, mem_norm, w_mem_kv, mq_norm, mk_norm, w_br_diff, w_br_pool, w_br_mem, w_out, norm2,
           w_router, b_router, w_gate_up, b_gate_up, w_down, b_down):
    B, S, D = x.shape
    depth = norm1.shape[0]
    bf = jnp.bfloat16
    qw = 2 * DIFF_HEADS * HEAD_DIM
    vw = DIFF_HEADS * 2 * HEAD_DIM
    pw = w_pool.shape[1] * w_pool.shape[2]
    mw = MEM_HEADS * HEAD_DIM
    k_off, v_off, pool_off, mq_off, gate_off = qw, 2 * qw, 2 * qw + vw, 2 * qw + vw + pw, 2 * qw + vw + pw + mw
    half = HEAD_DIM // 2

    hpd = (jnp.arange(2)[None, :, None] * (DIFF_HEADS * HEAD_DIM) + jnp.arange(DIFF_HEADS)[:, None, None] * HEAD_DIM
           + jnp.arange(HEAD_DIM)[None, None, :])
    perm = hpd.reshape(-1)
    perm_swapped = (hpd - hpd % HEAD_DIM + (hpd % HEAD_DIM + half) % HEAD_DIM).reshape(-1)
    cos, sin = _rope_tables(S)
    cos_t, sin_t = cos.T, sin.T
    cos_k = jnp.tile(cos, (1, 4))
    sin_k = jnp.tile(jnp.concatenate([-sin, sin], axis=1), (1, 2))
    gidx = jnp.arange(qw) // HEAD_DIM
    gmat = (gidx[:, None] == gidx[None, :]).astype(bf)
    pidx = jnp.arange(pw) // w_pool.shape[2]

    for l in range(depth):
        wi = w_in[l]
        wt = jnp.concatenate([wi[:, perm], wi[:, v_off:pool_off], wi[:, mq_off:gate_off]], axis=1).T.astype(bf)
        wr = jnp.concatenate([wi[:, k_off + perm], wi[:, k_off + perm_swapped], wi[:, pool_off:mq_off]],
                             axis=1).astype(bf)
        wg = wi[:, gate_off:].astype(bf)
        kg = jnp.tile(k_norm[l], 2)[None, :]
        kgs = jnp.tile(jnp.roll(k_norm[l], half), 2)[None, :]
        wpool_bd = jnp.where(pidx[:, None] == pidx[None, :],
                             jnp.tile(w_pool[l].reshape(pw, -1), (1, w_pool.shape[1])), 0.0).astype(bf)
        mk, mv_t = _mem_kv(mem, mem_norm[l][None, :], w_mem_kv[l].T.astype(bf), mk_norm[l][:, None])
        q_t, k, v_t, y_b, y_c = _in_proj(
            x, norm1[l][None, :], wt, wr, gmat, cos_t, sin_t, cos_k, sin_k, q_norm[l][:, None], kg, kgs,
            mq_norm[l][:, None], wpool_bd, pool_scale[l][None, :], mk, mv_t)
        y_a = _diff_attn(q_norm[l], k_norm[l], q_t, k, v_t, lambda_q1[l][None, :], lambda_k1[l][None, :], lambda_q2[l][None, :],
                         lambda_k2[l][None, :], diff_subln[l][:, None])
        wr_t = w_router[l].T
        wr_hi = wr_t.astype(bf)
        wr_lo = (wr_t - wr_hi.astype(jnp.float32)).astype(bf)
        x1, h2, top_idx, gate_w = _merge(
            x, norm1[l][None, :], wg, b_gate[l][None, :], y_a, y_b, y_c, w_br_diff[l].astype(bf),
            w_br_pool[l].astype(bf), w_br_mem[l].astype(bf), w_out[l].astype(bf), norm2[l][None, :], wr_hi, wr_lo,
            b_router[l][:, None])
        T = B * S
        top_idx = top_idx.reshape(TOP_K, T)
        gate_w = gate_w.reshape(TOP_K, T)
        nused, blk_exp, row_tok, row_dst, row_w, P = _dispatch(top_idx, gate_w, EXPERT_ROWS)
        y4 = _experts(nused, blk_exp, row_tok, row_dst, h2, row_w, w_gate_up[l],
                      b_gate_up[l][:, None, :], w_down[l], b_down[l][:, None, :], P)
        x = _combine(x1.reshape(T, D), y4).reshape(B, S, D)
    return x
```

```python
import functools
import math

import jax
import jax.numpy as jnp
from jax import lax
from jax.experimental import pallas as pl
from jax.experimental.pallas import tpu as pltpu

HEAD_DIM = 64
CHUNK = 64
RMS_EPS = 1e-6
ROPE_THETA = 10000.0
DIFF_HEADS = 4
MEM_HEADS = 4
POOL_WINDOWS = (2, 4, 8, 16)
POOL_HALO = 16
N_EXPERTS = 32
TOP_K = 4
SWIGLU_LIMIT = 7.0
SWIGLU_ALPHA = 1.702
NEG_INF = -1e30
LAM_INIT = 0.8 - 0.6 * math.exp(-0.3 * 0)
LOG2E = math.log2(math.e)
FIXED_OFFSET_LIMIT = 40.0 * LOG2E

SUBLANES, LANES = 8, 128
TOKEN_TILE = 512
EXPERT_ROWS = 256
VMEM_LIMIT = 56 * 1024 * 1024

_NT = (((1,), (1,)), ((), ()))


def _dot(a, b):
    return jnp.dot(a, b, preferred_element_type=jnp.float32)


def _dot_nt(a, b):
    return lax.dot_general(a, b, _NT, preferred_element_type=jnp.float32)


def _store_token_tiles(ref, x):
    n = x.shape[0]
    for c in range(SUBLANES):
        ref[pl.ds(c, n, stride=SUBLANES), :] = x[:, c * LANES:(c + 1) * LANES]


def _load_token_tiles(ref, n):
    return jnp.concatenate([ref[pl.ds(c, n, stride=SUBLANES), :] for c in range(SUBLANES)], axis=1)


def _rms_rows(x, gain_row):
    return x * lax.rsqrt(jnp.mean(x * x, axis=-1, keepdims=True) + RMS_EPS) * gain_row


def _rms_cols(x, gain_col):
    return x * lax.rsqrt(jnp.mean(x * x, axis=0, keepdims=True) + RMS_EPS) * gain_col


def _mem_kv_kernel(mem_ref, gain_ref, wkv_t_ref, mk_gain_ref, mk_ref, mv_t_ref):
    mem_n = _rms_rows(mem_ref[0], gain_ref[...]).astype(jnp.bfloat16)
    kv_t = _dot_nt(wkv_t_ref[...], mem_n)
    mw = kv_t.shape[0] // 2
    for h in range(MEM_HEADS):
        blk = kv_t[h * HEAD_DIM:(h + 1) * HEAD_DIM]
        mk_t = _rms_cols(blk, mk_gain_ref[...]) * (HEAD_DIM ** -0.5)
        mk_ref[0, h] = mk_t.T.astype(jnp.bfloat16)
    mv_t_ref[0] = kv_t[mw:].astype(jnp.bfloat16)


def _mem_kv(mem, mem_norm, wkv_t, mk_gain_col):
    B, M, D = mem.shape
    mw = wkv_t.shape[0] // 2
    return pl.pallas_call(
        _mem_kv_kernel,
        grid=(B,),
        in_specs=[
            pl.BlockSpec((1, M, D), lambda b: (b, 0, 0)),
            pl.BlockSpec((1, D), lambda b: (0, 0)),
            pl.BlockSpec((2 * mw, D), lambda b: (0, 0)),
            pl.BlockSpec((HEAD_DIM, 1), lambda b: (0, 0)),
        ],
        out_specs=[
            pl.BlockSpec((1, MEM_HEADS, M, HEAD_DIM), lambda b: (b, 0, 0, 0)),
            pl.BlockSpec((1, mw, M), lambda b: (b, 0, 0)),
        ],
        out_shape=[
            jax.ShapeDtypeStruct((B, MEM_HEADS, M, HEAD_DIM), jnp.bfloat16),
            jax.ShapeDtypeStruct((B, mw, M), jnp.bfloat16),
        ],
        name="mem_kv",
    )(mem, mem_norm, wkv_t, mk_gain_col)


def _in_proj_kernel(x_ref, norm1_ref, wt_ref, wr_ref, gmat_ref, cos_t_ref, sin_t_ref, cos_k_ref, sin_k_ref,
                    qg_ref, kg_ref, kgs_ref, mqg_ref, wpool_ref, pscale_ref, mk_ref, mv_t_ref,
                    q_t_ref, k_ref, v_t_ref, yb_ref, yc_ref, halo_ref, ext_ref):
    s_idx = pl.program_id(1)
    tm = x_ref.shape[1]
    h = _rms_rows(x_ref[0], norm1_ref[...]).astype(jnp.bfloat16)
    z_t = _dot_nt(wt_ref[...], h)
    z_r = _dot(h, wr_ref[...])

    cos_t, sin_t = cos_t_ref[...], sin_t_ref[...]
    half = HEAD_DIM // 2
    for g in range(2 * DIFF_HEADS):
        y = _rms_cols(z_t[g * HEAD_DIM:(g + 1) * HEAD_DIM], qg_ref[...]) * (HEAD_DIM ** -0.5 * LOG2E)
        t1, t2 = y[:half], y[half:]
        q_t_ref[0, g * HEAD_DIM:g * HEAD_DIM + half] = (t1 * cos_t - t2 * sin_t).astype(jnp.bfloat16)
        q_t_ref[0, g * HEAD_DIM + half:(g + 1) * HEAD_DIM] = (t2 * cos_t + t1 * sin_t).astype(jnp.bfloat16)

    qw = 2 * DIFF_HEADS * HEAD_DIM
    vd = 2 * HEAD_DIM
    for hd in range(DIFF_HEADS):
        v_t_ref[0, hd, 0] = z_t[qw + hd * vd:qw + (hd + 1) * vd].astype(jnp.bfloat16)

    mq_off = qw + DIFF_HEADS * vd
    outs = []
    for hd in range(MEM_HEADS):
        mq_t = _rms_cols(z_t[mq_off + hd * HEAD_DIM:mq_off + (hd + 1) * HEAD_DIM], mqg_ref[...])
        s_t = _dot(mk_ref[0, hd], mq_t.astype(jnp.bfloat16))
        p_t = jnp.exp(s_t - jnp.max(s_t, axis=0, keepdims=True))
        o_t = _dot(mv_t_ref[0, hd * HEAD_DIM:(hd + 1) * HEAD_DIM], p_t.astype(jnp.bfloat16))
        outs.append(o_t / jnp.sum(p_t, axis=0, keepdims=True))
    yc_ref[0] = jnp.concatenate(outs, axis=0).T.astype(jnp.bfloat16)

    kw = 2 * DIFF_HEADS * HEAD_DIM
    zk, zks = z_r[:, :kw], z_r[:, kw:2 * kw]
    ssq = _dot((zk * zk).astype(jnp.bfloat16), gmat_ref[...])
    r = lax.rsqrt(ssq * (1.0 / HEAD_DIM) + RMS_EPS)
    cos_k, sin_k = cos_k_ref[...], sin_k_ref[...]
    for j in range(kw // 128):
        sl = slice(j * 128, (j + 1) * 128)
        kk = r[:, sl] * (zk[:, sl] * kg_ref[...] * cos_k + zks[:, sl] * kgs_ref[...] * sin_k)
        k_ref[0, :, sl] = kk.astype(jnp.bfloat16)

    u = z_r[:, 2 * kw:]

    @pl.when(s_idx == 0)
    def _():
        halo_ref[...] = jnp.zeros_like(halo_ref)

    ext_ref[0:POOL_HALO] = halo_ref[...]
    ext_ref[POOL_HALO:] = u
    halo_ref[...] = u[tm - POOL_HALO:]
    pos1 = (s_idx * tm + 1 + lax.broadcasted_iota(jnp.int32, (tm, 128), 0)).astype(jnp.float32)
    lane = lax.broadcasted_iota(jnp.int32, (tm, 128), 1)
    pooled = []
    for part in range(2):
        cols = slice(part * 128, (part + 1) * 128)
        w_a, w_b = POOL_WINDOWS[2 * part], POOL_WINDOWS[2 * part + 1]
        acc = ext_ref[POOL_HALO:POOL_HALO + tm, cols]
        sum_a = None
        for j in range(1, w_b):
            if j == w_a:
                sum_a = acc
            acc = acc + ext_ref[POOL_HALO - j:POOL_HALO - j + tm, cols]
        mean_a = sum_a / jnp.minimum(pos1, float(w_a))
        mean_b = acc / jnp.minimum(pos1, float(w_b))
        pooled.append(jnp.where(lane < 64, mean_a, mean_b) - u[:, cols])
    pooled = jnp.concatenate(pooled, axis=1).astype(jnp.bfloat16)
    yb_ref[0] = (_dot(pooled, wpool_ref[...]) * pscale_ref[...]).astype(jnp.bfloat16)


def _in_proj(x, norm1, wt, wr, gmat, cos_t, sin_t, cos_k, sin_k, qg, kg, kgs, mqg, wpool, pscale, mk, mv_t):
    B, S, D = x.shape
    tm = TOKEN_TILE
    ns = S // tm
    nt, nr = wt.shape[0], wr.shape[1]
    qw = 2 * DIFF_HEADS * HEAD_DIM
    vd = 2 * HEAD_DIM
    pw = wpool.shape[0]
    M = mk.shape[2]
    mw = mv_t.shape[1]
    const = lambda b, s: (0, 0)
    return pl.pallas_call(
        _in_proj_kernel,
        grid=(B, ns),
        in_specs=[
            pl.BlockSpec((1, tm, D), lambda b, s: (b, s, 0)),
            pl.BlockSpec((1, D), const),
            pl.BlockSpec((nt, D), const),
            pl.BlockSpec((D, nr), const),
            pl.BlockSpec((qw, qw), const),
            pl.BlockSpec((HEAD_DIM // 2, tm), lambda b, s: (0, s)),
            pl.BlockSpec((HEAD_DIM // 2, tm), lambda b, s: (0, s)),
            pl.BlockSpec((tm, 128), lambda b, s: (s, 0)),
            pl.BlockSpec((tm, 128), lambda b, s: (s, 0)),
            pl.BlockSpec((HEAD_DIM, 1), const),
            pl.BlockSpec((1, 128), const),
            pl.BlockSpec((1, 128), const),
            pl.BlockSpec((HEAD_DIM, 1), const),
            pl.BlockSpec((pw, pw), const),
            pl.BlockSpec((1, pw), const),
            pl.BlockSpec((1, MEM_HEADS, M, HEAD_DIM), lambda b, s: (b, 0, 0, 0)),
            pl.BlockSpec((1, mw, M), lambda b, s: (b, 0, 0)),
        ],
        out_specs=[
            pl.BlockSpec((1, qw, tm), lambda b, s: (b, 0, s)),
            pl.BlockSpec((1, tm, qw), lambda b, s: (b, s, 0)),
            pl.BlockSpec((1, DIFF_HEADS, 1, vd, tm), lambda b, s: (b, 0, s, 0, 0)),
            pl.BlockSpec((1, tm, pw), lambda b, s: (b, s, 0)),
            pl.BlockSpec((1, tm, mw), lambda b, s: (b, s, 0)),
        ],
        out_shape=[
            jax.ShapeDtypeStruct((B, qw, S), jnp.bfloat16),
            jax.ShapeDtypeStruct((B, S, qw), jnp.bfloat16),
            jax.ShapeDtypeStruct((B, DIFF_HEADS, ns, vd, tm), jnp.bfloat16),
            jax.ShapeDtypeStruct((B, S, pw), jnp.bfloat16),
            jax.ShapeDtypeStruct((B, S, mw), jnp.bfloat16),
        ],
        scratch_shapes=[
            pltpu.VMEM((POOL_HALO, pw), jnp.float32),
            pltpu.VMEM((POOL_HALO + tm, pw), jnp.float32),
        ],
        compiler_params=pltpu.CompilerParams(
            dimension_semantics=("arbitrary", "arbitrary"), vmem_limit_bytes=VMEM_LIMIT),
        name="in_proj",
    )(x, norm1, wt, wr, gmat, cos_t, sin_t, cos_k, sin_k, qg, kg, kgs, mqg, wpool, pscale, mk, mv_t)


def _absmax(ref):
    return lax.fori_loop(0, ref.shape[0], lambda i, m: jnp.maximum(m, jnp.abs(ref[i])), jnp.float32(0.0))


def _diff_attn_kernel(qn_ref, kn_ref, q_t_ref, k_ref, v_t_ref, lq1_ref, lk1_ref, lq2_ref, lk2_ref, subg_ref, o_ref,
                      rhs_ref, m_ref, l_ref, acc_ref, sa_ref, sb_ref):
    qi = pl.program_id(2)
    tq = q_t_ref.shape[2]
    tk = tq
    q_t = q_t_ref[0]
    row = lax.broadcasted_iota(jnp.int32, q_t.shape, 0)
    zero = jnp.zeros_like(q_t)
    rhs_ref[0] = jnp.where(row < HEAD_DIM, q_t, zero)
    rhs_ref[1] = jnp.where(row >= HEAD_DIM, q_t, zero)
    l_ref[...] = jnp.zeros(l_ref.shape, jnp.float32)
    acc_ref[...] = jnp.zeros(acc_ref.shape, jnp.float32)

    bound = (HEAD_DIM ** 0.5) * LOG2E * _absmax(qn_ref) * _absmax(kn_ref)

    def blocks(j):
        k_blk = k_ref[0, pl.ds(pl.multiple_of(j * tk, tk), tk), :]
        v_blk = v_t_ref[0, 0, j]
        return k_blk, v_blk

    def chunk_mask():
        kc = lax.broadcasted_iota(jnp.int32, (tk, tq), 0) // CHUNK
        qc = lax.broadcasted_iota(jnp.int32, (tk, tq), 1) // CHUNK
        return kc <= qc

    def sweep(step):
        lax.fori_loop(0, qi, lambda j, c: (step(j, False), c)[1], 0)
        step(qi, True)

    @pl.when(bound <= FIXED_OFFSET_LIMIT)
    def _():
        def produce(j, s_ref):
            k_blk = k_ref[0, pl.ds(pl.multiple_of(j * tk, tk), tk), :]
            for c in range(2):
                s_ref[c] = _dot(k_blk, rhs_ref[c])

        def consume(j, s_ref, masked):
            v_blk = v_t_ref[0, 0, j]
            mask = chunk_mask() if masked else None
            for c in range(2):
                p = jnp.exp2(s_ref[c] - bound)
                if masked:
                    p = jnp.where(mask, p, 0.0)
                l_ref[c] += jnp.sum(p.reshape(tk // 8, 8, tq), axis=0)
                acc_ref[c] += _dot(v_blk, p.astype(jnp.bfloat16))

        produce(0, sa_ref)

        def pair(i, carry):
            produce(2 * i + 1, sb_ref)
            consume(2 * i, sa_ref, False)
            produce(2 * i + 2, sa_ref)
            consume(2 * i + 1, sb_ref, False)
            return carry

        lax.fori_loop(0, qi // 2, pair, 0)

        @pl.when(qi % 2 == 0)
        def _():
            consume(qi, sa_ref, True)

        @pl.when(qi % 2 == 1)
        def _():
            produce(qi, sb_ref)
            consume(qi - 1, sa_ref, False)
            consume(qi, sb_ref, True)

    @pl.when(bound > FIXED_OFFSET_LIMIT)
    def _():
        m_ref[...] = jnp.full(m_ref.shape, NEG_INF, jnp.float32)

        def step(j, masked):
            k_blk, v_blk = blocks(j)
            mask = chunk_mask() if masked else None
            for c in range(2):
                s = _dot(k_blk, rhs_ref[c])
                if masked:
                    s = jnp.where(mask, s, NEG_INF)
                m_prev = m_ref[c]
                m_new = jnp.maximum(m_prev, jnp.max(s, axis=0, keepdims=True))
                alpha = jnp.exp2(m_prev - m_new)
                p = jnp.exp2(s - m_new)
                l_ref[c, 0:1] = alpha * l_ref[c, 0:1] + jnp.sum(p, axis=0, keepdims=True)
                acc_ref[c] = alpha * acc_ref[c] + _dot(v_blk, p.astype(jnp.bfloat16))
                m_ref[c] = m_new
        sweep(step)

    lam = (jnp.exp(jnp.sum(lq1_ref[...] * lk1_ref[...], axis=-1, keepdims=True))
           - jnp.exp(jnp.sum(lq2_ref[...] * lk2_ref[...], axis=-1, keepdims=True)) + LAM_INIT)
    l0 = jnp.sum(l_ref[0], axis=0, keepdims=True)
    l1 = jnp.sum(l_ref[1], axis=0, keepdims=True)
    o = acc_ref[0] / l0 - lam * (acc_ref[1] / l1)
    y = _rms_cols(o, subg_ref[...]) * (1.0 - LAM_INIT)
    o_ref[0] = y.T.astype(jnp.bfloat16)


def _diff_attn(q_norm, k_norm, q_t, k, v_t, lq1, lk1, lq2, lk2, subg_col):
    B, qw, S = q_t.shape
    tq = TOKEN_TILE
    nq = S // tq
    vd = 2 * HEAD_DIM
    lam_spec = pl.BlockSpec((1, HEAD_DIM), lambda b, h, i: (0, 0))
    smem = pl.BlockSpec(memory_space=pltpu.SMEM)
    return pl.pallas_call(
        _diff_attn_kernel,
        grid=(B, DIFF_HEADS, nq),
        in_specs=[
            smem, smem,
            pl.BlockSpec((1, vd, tq), lambda b, h, i: (b, h, i)),
            pl.BlockSpec((1, S, vd), lambda b, h, i: (b, 0, h)),
            pl.BlockSpec((1, 1, nq, vd, tq), lambda b, h, i: (b, h, 0, 0, 0)),
            lam_spec, lam_spec, lam_spec, lam_spec,
            pl.BlockSpec((vd, 1), lambda b, h, i: (0, 0)),
        ],
        out_specs=pl.BlockSpec((1, tq, vd), lambda b, h, i: (b, i, h)),
        out_shape=jax.ShapeDtypeStruct((B, S, DIFF_HEADS * vd), jnp.bfloat16),
        scratch_shapes=[
            pltpu.VMEM((2, vd, tq), jnp.bfloat16),
            pltpu.VMEM((2, 1, tq), jnp.float32),
            pltpu.VMEM((2, 8, tq), jnp.float32),
            pltpu.VMEM((2, vd, tq), jnp.float32),
            pltpu.VMEM((2, tq, tq), jnp.float32),
            pltpu.VMEM((2, tq, tq), jnp.float32),
        ],
        compiler_params=pltpu.CompilerParams(
            dimension_semantics=("arbitrary", "arbitrary", "arbitrary"), vmem_limit_bytes=VMEM_LIMIT),
        name="diff_attn",
    )(q_norm, k_norm, q_t, k, v_t, lq1, lk1, lq2, lk2, subg_col)


def _merge_kernel(x_ref, norm1_ref, wg_ref, bg_ref, ya_ref, yb_ref, yc_ref, wa_ref, wb_ref, wc_ref, wo_ref,
                  norm2_ref, wr_hi_ref, wr_lo_ref, br_ref, x1_ref, h2_ref, idx_ref, gw_ref):
    D = x_ref.shape[2]
    x = x_ref[0]
    h = _rms_rows(x, norm1_ref[...]).astype(jnp.bfloat16)
    merged = None
    for i, (y_ref, w_ref) in enumerate(((ya_ref, wa_ref), (yb_ref, wb_ref), (yc_ref, wc_ref))):
        gz = _dot(h, wg_ref[:, i * D:(i + 1) * D]) + bg_ref[:, i * D:(i + 1) * D]
        gate = 1.0 / (1.0 + jnp.exp(-gz))
        term = gate * _dot(y_ref[0], w_ref[...])
        merged = term if merged is None else merged + term
    x1 = x + _dot(merged.astype(jnp.bfloat16), wo_ref[...])
    x1_ref[0] = x1
    h2 = _rms_rows(x1, norm2_ref[...])
    _store_token_tiles(h2_ref, h2)

    hi = h2.astype(jnp.bfloat16)
    lo = (h2 - hi.astype(jnp.float32)).astype(jnp.bfloat16)
    logits = (_dot_nt(wr_hi_ref[...], hi) + _dot_nt(wr_hi_ref[...], lo) + _dot_nt(wr_lo_ref[...], hi)
              + br_ref[...])
    eidx = lax.broadcasted_iota(jnp.int32, logits.shape, 0)
    vals = logits
    top_v, top_i = [], []
    for _ in range(TOP_K):
        mx = jnp.max(vals, axis=0, keepdims=True)
        ix = jnp.min(jnp.where(vals == mx, eidx, N_EXPERTS), axis=0, keepdims=True)
        top_v.append(mx)
        top_i.append(ix)
        vals = jnp.where(eidx == ix, -jnp.inf, vals)
    ex = [jnp.exp(v - top_v[0]) for v in top_v]
    den = ex[0] + ex[1] + ex[2] + ex[3]
    for k in range(TOP_K):
        idx_ref[k, 0] = top_i[k]
        gw_ref[k, 0] = ex[k] / den


def _merge(x, norm1, wg, bg, ya, yb, yc, wa, wb, wc, wo, norm2, wr_hi, wr_lo, br_col):
    B, S, D = x.shape
    tm = TOKEN_TILE
    const = lambda b, s: (0, 0)
    tok = lambda b, s: (b, s, 0)
    full = lambda a: pl.BlockSpec(a.shape, const)
    return pl.pallas_call(
        _merge_kernel,
        grid=(B, S // tm),
        in_specs=[
            pl.BlockSpec((1, tm, D), tok), full(norm1), full(wg), full(bg),
            pl.BlockSpec((1, tm, ya.shape[2]), tok), pl.BlockSpec((1, tm, yb.shape[2]), tok),
            pl.BlockSpec((1, tm, yc.shape[2]), tok),
            full(wa), full(wb), full(wc), full(wo), full(norm2), full(wr_hi), full(wr_lo), full(br_col),
        ],
        out_specs=[
            pl.BlockSpec((1, tm, D), tok),
            pl.BlockSpec((tm * SUBLANES, LANES), lambda b, s: (b * (S // tm) + s, 0)),
            pl.BlockSpec((TOP_K, 1, 1, tm), lambda b, s: (0, b * (S // tm) + s, 0, 0)),
            pl.BlockSpec((TOP_K, 1, 1, tm), lambda b, s: (0, b * (S // tm) + s, 0, 0)),
        ],
        out_shape=[
            jax.ShapeDtypeStruct((B, S, D), jnp.float32),
            jax.ShapeDtypeStruct((B * S * SUBLANES, LANES), jnp.float32),
            jax.ShapeDtypeStruct((TOP_K, B * S // tm, 1, tm), jnp.int32),
            jax.ShapeDtypeStruct((TOP_K, B * S // tm, 1, tm), jnp.float32),
        ],
        compiler_params=pltpu.CompilerParams(
            dimension_semantics=("arbitrary", "arbitrary"), vmem_limit_bytes=VMEM_LIMIT),
        name="merge",
    )(x, norm1, wg, bg, ya, yb, yc, wa, wb, wc, wo, norm2, wr_hi, wr_lo, br_col)


def _experts_kernel(nused_ref, bexp_ref, tok_cur_ref, tok_nxt_ref, dst_ref, h2_hbm, roww_ref, wgu_ref, bgu_ref,
                    wd_ref, bd_ref, y4_hbm, xbuf, ybuf, wgu_bf, wd_bf, gsem, ssem):
    b = pl.program_id(0)
    nb = pl.num_programs(0)
    nused = nused_ref[0]
    rows = xbuf.shape[1] // SUBLANES
    F = wd_ref.shape[1]
    slot = b % 2

    def tile(ref, off):
        return ref.at[pl.ds(pl.multiple_of(off, SUBLANES), SUBLANES)]

    def gather(tok_ref, buf_slot):
        for i in range(rows):
            pltpu.make_async_copy(tile(h2_hbm, tok_ref[0, 0, i]), xbuf.at[buf_slot, pl.ds(i * SUBLANES, SUBLANES)],
                                  gsem.at[buf_slot]).start()

    def wait_rows(buf, sem):
        pltpu.make_async_copy(buf, buf, sem).wait()

    @pl.when(b == 0)
    def _():
        gather(tok_cur_ref, 0)
        ybuf[...] = jnp.zeros_like(ybuf)
        n_real = y4_hbm.shape[0] - 2 * rows * SUBLANES
        for s in range(2):
            spare = pltpu.make_async_copy(
                ybuf.at[s], y4_hbm.at[pl.ds(n_real + s * rows * SUBLANES, rows * SUBLANES)], ssem.at[s])
            spare.start()
            spare.wait()

    @pl.when(jnp.logical_and(b >= 2, b - 2 < nused))
    def _():
        wait_rows(ybuf.at[slot], ssem.at[slot])

    changed = jnp.logical_or(b == 0, bexp_ref[b] != bexp_ref[jnp.maximum(b - 1, 0)])

    @pl.when(jnp.logical_and(changed, b < nused))
    def _():
        chunk = 128
        def cast(i, carry):
            r = pl.ds(pl.multiple_of(i * chunk, chunk), chunk)
            wgu_bf[r, :] = wgu_ref[0, r, :].astype(jnp.bfloat16)
            wd_bf[r, :] = wd_ref[0, r, :].astype(jnp.bfloat16)
            return carry
        lax.fori_loop(0, wgu_bf.shape[0] // chunk, cast, 0)

    @pl.when(jnp.logical_or(b == 0, b <= nused))
    def _():
        wait_rows(xbuf.at[slot], gsem.at[slot])

    @pl.when(b < nused)
    def _():
        gather(tok_nxt_ref, 1 - slot)
        x = _load_token_tiles(xbuf.at[slot], rows).astype(jnp.bfloat16)
        gu = _dot(x, wgu_bf[...]) + bgu_ref[0]
        gate = jnp.minimum(gu[:, :F], SWIGLU_LIMIT)
        up = jnp.clip(gu[:, F:], -SWIGLU_LIMIT, SWIGLU_LIMIT)
        act = gate * (1.0 / (1.0 + jnp.exp(-SWIGLU_ALPHA * gate))) * (up + 1.0)
        y = _dot(act.astype(jnp.bfloat16), wd_bf[...]) + bd_ref[0]
        w_row = roww_ref[0]
        eye = lax.broadcasted_iota(jnp.int32, (rows, rows), 0) == lax.broadcasted_iota(jnp.int32, (rows, rows), 1)
        w_hi = w_row.astype(jnp.bfloat16).astype(jnp.float32)
        ones = jnp.ones((rows, LANES), jnp.bfloat16)
        w_col = (_dot(jnp.where(eye, w_hi, 0.0).astype(jnp.bfloat16), ones)
                 + _dot(jnp.where(eye, w_row - w_hi, 0.0).astype(jnp.bfloat16), ones))
        _store_token_tiles(ybuf.at[slot], y * jnp.concatenate([w_col] * (y.shape[1] // LANES), axis=1))
        for i in range(rows):
            pltpu.make_async_copy(ybuf.at[slot, pl.ds(i * SUBLANES, SUBLANES)], tile(y4_hbm, dst_ref[0, 0, i]),
                                  ssem.at[slot]).start()

    @pl.when(b == nb - 1)
    def _():
        @pl.when(jnp.logical_and(b >= 1, b - 1 < nused))
        def _():
            wait_rows(ybuf.at[1 - slot], ssem.at[1 - slot])

        @pl.when(b < nused)
        def _():
            wait_rows(ybuf.at[slot], ssem.at[slot])
            wait_rows(xbuf.at[1 - slot], gsem.at[1 - slot])


def _experts(nused, blk_exp, row_tok, row_dst, h2, row_w, wgu, bgu, wd, bd, n_out_rows):
    D = wgu.shape[1]
    nb = row_tok.shape[0]
    rows = row_tok.shape[2]
    F = wd.shape[1]
    grid_spec = pltpu.PrefetchScalarGridSpec(
        num_scalar_prefetch=2,
        grid=(nb,),
        in_specs=[
            pl.BlockSpec((1, 1, rows), lambda b, nu, be: (b, 0, 0), memory_space=pltpu.SMEM),
            pl.BlockSpec((1, 1, rows), lambda b, nu, be: (jnp.minimum(b + 1, nb - 1), 0, 0), memory_space=pltpu.SMEM),
            pl.BlockSpec((1, 1, rows), lambda b, nu, be: (b, 0, 0), memory_space=pltpu.SMEM),
            pl.BlockSpec(memory_space=pl.ANY),
            pl.BlockSpec((1, 1, rows), lambda b, nu, be: (b, 0, 0)),
            pl.BlockSpec((1, D, 2 * F), lambda b, nu, be: (be[b], 0, 0)),
            pl.BlockSpec((1, 1, 2 * F), lambda b, nu, be: (be[b], 0, 0)),
            pl.BlockSpec((1, F, D), lambda b, nu, be: (be[b], 0, 0)),
            pl.BlockSpec((1, 1, D), lambda b, nu, be: (be[b], 0, 0)),
        ],
        out_specs=pl.BlockSpec(memory_space=pl.ANY),
        scratch_shapes=[
            pltpu.VMEM((2, rows * SUBLANES, LANES), jnp.float32),
            pltpu.VMEM((2, rows * SUBLANES, LANES), jnp.float32),
            pltpu.VMEM((D, 2 * F), jnp.bfloat16),
            pltpu.VMEM((F, D), jnp.bfloat16),
            pltpu.SemaphoreType.DMA((2,)),
            pltpu.SemaphoreType.DMA((2,)),
        ],
    )
    return pl.pallas_call(
        _experts_kernel,
        grid_spec=grid_spec,
        out_shape=jax.ShapeDtypeStruct((n_out_rows * SUBLANES, LANES), jnp.float32),
        compiler_params=pltpu.CompilerParams(
            dimension_semantics=("arbitrary",), vmem_limit_bytes=VMEM_LIMIT),
        name="experts",
    )(nused, blk_exp, row_tok, row_tok, row_dst, h2, row_w, wgu, bgu, wd, bd)


def _combine_kernel(x1_ref, y0_ref, y1_ref, y2_ref, y3_ref, o_ref):
    tm = x1_ref.shape[0]
    for c in range(SUBLANES):
        rows = pl.ds(c, tm, stride=SUBLANES)
        cols = slice(c * LANES, (c + 1) * LANES)
        o_ref[:, cols] = x1_ref[:, cols] + (((y0_ref[rows, :] + y1_ref[rows, :]) + y2_ref[rows, :]) + y3_ref[rows, :])


def _combine(x1, y4):
    T, D = x1.shape
    tm = TOKEN_TILE
    nt = T // tm
    return pl.pallas_call(
        _combine_kernel,
        grid=(nt,),
        in_specs=[pl.BlockSpec((tm, D), lambda i: (i, 0))]
        + [pl.BlockSpec((tm * SUBLANES, LANES), functools.partial(lambda i, k: (k * nt + i, 0), k=k))
           for k in range(TOP_K)],
        out_specs=pl.BlockSpec((tm, D), lambda i: (i, 0)),
        out_shape=jax.ShapeDtypeStruct((T, D), jnp.float32),
        name="combine",
    )(x1, y4, y4, y4, y4)


def _rope_tables(seq):
    inv_freq = 1.0 / (ROPE_THETA ** (jnp.arange(0, HEAD_DIM, 2, dtype=jnp.float32) / HEAD_DIM))
    ang = jnp.arange(seq, dtype=jnp.float32)[:, None] * inv_freq[None, :]
    return jnp.cos(ang), jnp.sin(ang)


def _dispatch(top_idx, gate_w, rows):
    T = top_idx.shape[1]
    A = T * TOP_K
    e_flat = top_idx.reshape(A)
    order = jnp.argsort(e_flat).astype(jnp.int32)
    experts = jnp.arange(N_EXPERTS, dtype=jnp.int32)
    counts = jnp.sum((e_flat[:, None] == experts[None, :]).astype(jnp.int32), axis=0)
    padded = ((counts + rows - 1) // rows) * rows
    pend = jnp.cumsum(padded)
    pstart = pend - padded
    ustart = jnp.cumsum(counts) - counts
    P = A + N_EXPERTS * rows
    nb = P // rows
    blk_start = jnp.arange(nb, dtype=jnp.int32) * rows
    blk_exp = jnp.minimum(jnp.sum((pend[None, :] <= blk_start[:, None]).astype(jnp.int32), axis=1), N_EXPERTS - 1)
    in_blk = jnp.arange(rows, dtype=jnp.int32)[None, :]
    j = blk_start[:, None] + in_blk - pstart[blk_exp][:, None]
    valid = j < counts[blk_exp][:, None]
    a_flat = jnp.take(order, jnp.clip(ustart[blk_exp][:, None] + j, 0, A - 1).reshape(P))
    w_flat = jnp.take(gate_w.reshape(A), a_flat).reshape(nb, rows)
    a = a_flat.reshape(nb, rows)
    row_tok = jnp.where(valid, a % T, 0)
    row_dst = jnp.where(valid, a, A + (jnp.arange(nb, dtype=jnp.int32)[:, None] % 2) * rows + in_blk)
    row_w = jnp.where(valid, w_flat, 0.0)
    nused = (pend[-1] // rows).astype(jnp.int32).reshape(1)
    return (nused, blk_exp, (row_tok * SUBLANES).reshape(nb, 1, rows), (row_dst * SUBLANES).reshape(nb, 1, rows),
            row_w.reshape(nb, 1, rows), A + 2 * rows)


def kernel(x, mem, norm1, w_in, b_gate, q_norm, k_norm, lambda_q1, lambda_k1, lambda_q2, lambda_k2, diff_subln,
           w_pool, pool_scale, mem_norm, w_mem_kv, mq_norm, mk_norm, w_br_diff, w_br_pool, w_br_mem, w_out, norm2,
           w_router, b_router, w_gate_up, b_gate_up, w_down, b_down):
    B, S, D = x.shape
    depth = norm1.shape[0]
    bf = jnp.bfloat16
    qw = 2 * DIFF_HEADS * HEAD_DIM
    vw = DIFF_HEADS * 2 * HEAD_DIM
    pw = w_pool.shape[1] * w_pool.shape[2]
    mw = MEM_HEADS * HEAD_DIM
    k_off, v_off, pool_off, mq_off, gate_off = qw, 2 * qw, 2 * qw + vw, 2 * qw + vw + pw, 2 * qw + vw + pw + mw
    half = HEAD_DIM // 2

    hpd = (jnp.arange(2)[None, :, None] * (DIFF_HEADS * HEAD_DIM) + jnp.arange(DIFF_HEADS)[:, None, None] * HEAD_DIM
           + jnp.arange(HEAD_DIM)[None, None, :])
    perm = hpd.reshape(-1)
    perm_swapped = (hpd - hpd % HEAD_DIM + (hpd % HEAD_DIM + half) % HEAD_DIM).reshape(-1)
    cos, sin = _rope_tables(S)
    cos_t, sin_t = cos.T, sin.T
    cos_k = jnp.tile(cos, (1, 4))
    sin_k = jnp.tile(jnp.concatenate([-sin, sin], axis=1), (1, 2))
    gidx = jnp.arange(qw) // HEAD_DIM
    gmat = (gidx[:, None] == gidx[None, :]).astype(bf)
    pidx = jnp.arange(pw) // w_pool.shape[2]

    for l in range(depth):
        wi = w_in[l]
        wt = jnp.concatenate([wi[:, perm], wi[:, v_off:pool_off], wi[:, mq_off:gate_off]], axis=1).T.astype(bf)
        wr = jnp.concatenate([wi[:, k_off + perm], wi[:, k_off + perm_swapped], wi[:, pool_off:mq_off]],
                             axis=1).astype(bf)
        wg = wi[:, gate_off:].astype(bf)
        kg = jnp.tile(k_norm[l], 2)[None, :]
        kgs = jnp.tile(jnp.roll(k_norm[l], half), 2)[None, :]
        wpool_bd = jnp.where(pidx[:, None] == pidx[None, :],
                             jnp.tile(w_pool[l].reshape(pw, -1), (1, w_pool.shape[1])), 0.0).astype(bf)
        mk, mv_t = _mem_kv(mem, mem_norm[l][None, :], w_mem_kv[l].T.astype(bf), mk_norm[l][:, None])
        q_t, k, v_t, y_b, y_c = _in_proj(
            x, norm1[l][None, :], wt, wr, gmat, cos_t, sin_t, cos_k, sin_k, q_norm[l][:, None], kg, kgs,
            mq_norm[l][:, None], wpool_bd, pool_scale[l][None, :], mk, mv_t)
        y_a = _diff_attn(q_norm[l], k_norm[l], q_t, k, v_t, lambda_q1[l][None, :], lambda_k1[l][None, :], lambda_q2[l][None, :],
                         lambda_k2[l][None, :], diff_subln[l][:, None])
        wr_t = w_router[l].T
        wr_hi = wr_t.astype(bf)
        wr_lo = (wr_t - wr_hi.astype(jnp.float32)).astype(bf)
        x1, h2, top_idx, gate_w = _merge(
            x, norm1[l][None, :], wg, b_gate[l][None, :], y_a, y_b, y_c, w_br_diff[l].astype(bf),
            w_br_pool[l].astype(bf), w_br_mem[l].astype(bf), w_out[l].astype(bf), norm2[l][None, :], wr_hi, wr_lo,
            b_router[l][:, None])
        T = B * S
        top_idx = top_idx.reshape(TOP_K, T)
        gate_w = gate_w.reshape(TOP_K, T)
        nused, blk_exp, row_tok, row_dst, row_w, P = _dispatch(top_idx, gate_w, EXPERT_ROWS)
        y4 = _experts(nused, blk_exp, row_tok, row_dst, h2, row_w, w_gate_up[l],
                      b_gate_up[l][:, None, :], w_down[l], b_down[l][:, None, :], P)
        x = _combine(x1.reshape(T, D), y4).reshape(B, S, D)
    return x
```

```python
import functools
import math

import jax
import jax.numpy as jnp
from jax import lax
from jax.experimental import pallas as pl
from jax.experimental.pallas import tpu as pltpu

HEAD_DIM = 64
CHUNK = 64
RMS_EPS = 1e-6
ROPE_THETA = 10000.0
DIFF_HEADS = 4
MEM_HEADS = 4
POOL_WINDOWS = (2, 4, 8, 16)
POOL_HALO = 16
N_EXPERTS = 32
TOP_K = 4
SWIGLU_LIMIT = 7.0
SWIGLU_ALPHA = 1.702
NEG_INF = -1e30
LAM_INIT = 0.8 - 0.6 * math.exp(-0.3 * 0)
LOG2E = math.log2(math.e)
FIXED_OFFSET_LIMIT = 40.0 * LOG2E

SUBLANES, LANES = 8, 128
TOKEN_TILE = 512
EXPERT_ROWS = 256
VMEM_LIMIT = 56 * 1024 * 1024

_NT = (((1,), (1,)), ((), ()))


def _dot(a, b):
    return jnp.dot(a, b, preferred_element_type=jnp.float32)


def _dot_nt(a, b):
    return lax.dot_general(a, b, _NT, preferred_element_type=jnp.float32)


def _store_token_tiles(ref, x):
    n = x.shape[0]
    for c in range(SUBLANES):
        ref[pl.ds(c, n, stride=SUBLANES), :] = x[:, c * LANES:(c + 1) * LANES]


def _load_token_tiles(ref, n):
    return jnp.concatenate([ref[pl.ds(c, n, stride=SUBLANES), :] for c in range(SUBLANES)], axis=1)


def _rms_rows(x, gain_row):
    return x * lax.rsqrt(jnp.mean(x * x, axis=-1, keepdims=True) + RMS_EPS) * gain_row


def _rms_cols(x, gain_col):
    return x * lax.rsqrt(jnp.mean(x * x, axis=0, keepdims=True) + RMS_EPS) * gain_col


def _mem_kv_kernel(mem_ref, gain_ref, wkv_t_ref, mk_gain_ref, mk_ref, mv_t_ref):
    mem_n = _rms_rows(mem_ref[0], gain_ref[...]).astype(jnp.bfloat16)
    kv_t = _dot_nt(wkv_t_ref[...], mem_n)
    mw = kv_t.shape[0] // 2
    for h in range(MEM_HEADS):
        blk = kv_t[h * HEAD_DIM:(h + 1) * HEAD_DIM]
        mk_t = _rms_cols(blk, mk_gain_ref[...]) * (HEAD_DIM ** -0.5)
        mk_ref[0, h] = mk_t.T.astype(jnp.bfloat16)
    mv_t_ref[0] = kv_t[mw:].astype(jnp.bfloat16)


def _mem_kv(mem, mem_norm, wkv_t, mk_gain_col):
    B, M, D = mem.shape
    mw = wkv_t.shape[0] // 2
    return pl.pallas_call(
        _mem_kv_kernel,
        grid=(B,),
        in_specs=[
            pl.BlockSpec((1, M, D), lambda b: (b, 0, 0)),
            pl.BlockSpec((1, D), lambda b: (0, 0)),
            pl.BlockSpec((2 * mw, D), lambda b: (0, 0)),
            pl.BlockSpec((HEAD_DIM, 1), lambda b: (0, 0)),
        ],
        out_specs=[
            pl.BlockSpec((1, MEM_HEADS, M, HEAD_DIM), lambda b: (b, 0, 0, 0)),
            pl.BlockSpec((1, mw, M), lambda b: (b, 0, 0)),
        ],
        out_shape=[
            jax.ShapeDtypeStruct((B, MEM_HEADS, M, HEAD_DIM), jnp.bfloat16),
            jax.ShapeDtypeStruct((B, mw, M), jnp.bfloat16),
        ],
        name="mem_kv",
    )(mem, mem_norm, wkv_t, mk_gain_col)


def _in_proj_kernel(x_ref, norm1_ref, wt_ref, wr_ref, gmat_ref, cos_t_ref, sin_t_ref, cos_k_ref, sin_k_ref,
                    qg_ref, kg_ref, kgs_ref, mqg_ref, wpool_ref, pscale_ref, mk_ref, mv_t_ref,
                    q_t_ref, k_ref, v_t_ref, yb_ref, yc_ref, halo_ref, ext_ref):
    s_idx = pl.program_id(1)
    tm = x_ref.shape[1]
    h = _rms_rows(x_ref[0], norm1_ref[...]).astype(jnp.bfloat16)
    z_t = _dot_nt(wt_ref[...], h)
    z_r = _dot(h, wr_ref[...])

    cos_t, sin_t = cos_t_ref[...], sin_t_ref[...]
    half = HEAD_DIM // 2
    for g in range(2 * DIFF_HEADS):
        y = _rms_cols(z_t[g * HEAD_DIM:(g + 1) * HEAD_DIM], qg_ref[...]) * (HEAD_DIM ** -0.5 * LOG2E)
        t1, t2 = y[:half], y[half:]
        q_t_ref[0, g * HEAD_DIM:g * HEAD_DIM + half] = (t1 * cos_t - t2 * sin_t).astype(jnp.bfloat16)
        q_t_ref[0, g * HEAD_DIM + half:(g + 1) * HEAD_DIM] = (t2 * cos_t + t1 * sin_t).astype(jnp.bfloat16)

    qw = 2 * DIFF_HEADS * HEAD_DIM
    vd = 2 * HEAD_DIM
    for hd in range(DIFF_HEADS):
        v_t_ref[0, hd, 0] = z_t[qw + hd * vd:qw + (hd + 1) * vd].astype(jnp.bfloat16)

    mq_off = qw + DIFF_HEADS * vd
    outs = []
    for hd in range(MEM_HEADS):
        mq_t = _rms_cols(z_t[mq_off + hd * HEAD_DIM:mq_off + (hd + 1) * HEAD_DIM], mqg_ref[...])
        s_t = _dot(mk_ref[0, hd], mq_t.astype(jnp.bfloat16))
        p_t = jnp.exp(s_t - jnp.max(s_t, axis=0, keepdims=True))
        o_t = _dot(mv_t_ref[0, hd * HEAD_DIM:(hd + 1) * HEAD_DIM], p_t.astype(jnp.bfloat16))
        outs.append(o_t / jnp.sum(p_t, axis=0, keepdims=True))
    yc_ref[0] = jnp.concatenate(outs, axis=0).T.astype(jnp.bfloat16)

    kw = 2 * DIFF_HEADS * HEAD_DIM
    zk, zks = z_r[:, :kw], z_r[:, kw:2 * kw]
    ssq = _dot((zk * zk).astype(jnp.bfloat16), gmat_ref[...])
    r = lax.rsqrt(ssq * (1.0 / HEAD_DIM) + RMS_EPS)
    cos_k, sin_k = cos_k_ref[...], sin_k_ref[...]
    for j in range(kw // 128):
        sl = slice(j * 128, (j + 1) * 128)
        kk = r[:, sl] * (zk[:, sl] * kg_ref[...] * cos_k + zks[:, sl] * kgs_ref[...] * sin_k)
        k_ref[0, :, sl] = kk.astype(jnp.bfloat16)

    u = z_r[:, 2 * kw:]

    @pl.when(s_idx == 0)
    def _():
        halo_ref[...] = jnp.zeros_like(halo_ref)

    ext_ref[0:POOL_HALO] = halo_ref[...]
    ext_ref[POOL_HALO:] = u
    halo_ref[...] = u[tm - POOL_HALO:]
    pos1 = (s_idx * tm + 1 + lax.broadcasted_iota(jnp.int32, (tm, 128), 0)).astype(jnp.float32)
    lane = lax.broadcasted_iota(jnp.int32, (tm, 128), 1)
    pooled = []
    for part in range(2):
        cols = slice(part * 128, (part + 1) * 128)
        w_a, w_b = POOL_WINDOWS[2 * part], POOL_WINDOWS[2 * part + 1]
        acc = ext_ref[POOL_HALO:POOL_HALO + tm, cols]
        sum_a = None
        for j in range(1, w_b):
            if j == w_a:
                sum_a = acc
            acc = acc + ext_ref[POOL_HALO - j:POOL_HALO - j + tm, cols]
        mean_a = sum_a / jnp.minimum(pos1, float(w_a))
        mean_b = acc / jnp.minimum(pos1, float(w_b))
        pooled.append(jnp.where(lane < 64, mean_a, mean_b) - u[:, cols])
    pooled = jnp.concatenate(pooled, axis=1).astype(jnp.bfloat16)
    yb_ref[0] = (_dot(pooled, wpool_ref[...]) * pscale_ref[...]).astype(jnp.bfloat16)


def _in_proj(x, norm1, wt, wr, gmat, cos_t, sin_t, cos_k, sin_k, qg, kg, kgs, mqg, wpool, pscale, mk, mv_t):
    B, S, D = x.shape
    tm = TOKEN_TILE
    ns = S // tm
    nt, nr = wt.shape[0], wr.shape[1]
    qw = 2 * DIFF_HEADS * HEAD_DIM
    vd = 2 * HEAD_DIM
    pw = wpool.shape[0]
    M = mk.shape[2]
    mw = mv_t.shape[1]
    const = lambda b, s: (0, 0)
    return pl.pallas_call(
        _in_proj_kernel,
        grid=(B, ns),
        in_specs=[
            pl.BlockSpec((1, tm, D), lambda b, s: (b, s, 0)),
            pl.BlockSpec((1, D), const),
            pl.BlockSpec((nt, D), const),
            pl.BlockSpec((D, nr), const),
            pl.BlockSpec((qw, qw), const),
            pl.BlockSpec((HEAD_DIM // 2, tm), lambda b, s: (0, s)),
            pl.BlockSpec((HEAD_DIM // 2, tm), lambda b, s: (0, s)),
            pl.BlockSpec((tm, 128), lambda b, s: (s, 0)),
            pl.BlockSpec((tm, 128), lambda b, s: (s, 0)),
            pl.BlockSpec((HEAD_DIM, 1), const),
            pl.BlockSpec((1, 128), const),
            pl.BlockSpec((1, 128), const),
            pl.BlockSpec((HEAD_DIM, 1), const),
            pl.BlockSpec((pw, pw), const),
            pl.BlockSpec((1, pw), const),
            pl.BlockSpec((1, MEM_HEADS, M, HEAD_DIM), lambda b, s: (b, 0, 0, 0)),
            pl.BlockSpec((1, mw, M), lambda b, s: (b, 0, 0)),
        ],
        out_specs=[
            pl.BlockSpec((1, qw, tm), lambda b, s: (b, 0, s)),
            pl.BlockSpec((1, tm, qw), lambda b, s: (b, s, 0)),
            pl.BlockSpec((1, DIFF_HEADS, 1, vd, tm), lambda b, s: (b, 0, s, 0, 0)),
            pl.BlockSpec((1, tm, pw), lambda b, s: (b, s, 0)),
            pl.BlockSpec((1, tm, mw), lambda b, s: (b, s, 0)),
        ],
        out_shape=[
            jax.ShapeDtypeStruct((B, qw, S), jnp.bfloat16),
            jax.ShapeDtypeStruct((B, S, qw), jnp.bfloat16),
            jax.ShapeDtypeStruct((B, DIFF_HEADS, ns, vd, tm), jnp.bfloat16),
            jax.ShapeDtypeStruct((B, S, pw), jnp.bfloat16),
            jax.ShapeDtypeStruct((B, S, mw), jnp.bfloat16),
        ],
        scratch_shapes=[
            pltpu.VMEM((POOL_HALO, pw), jnp.float32),
            pltpu.VMEM((POOL_HALO + tm, pw), jnp.float32),
        ],
        compiler_params=pltpu.CompilerParams(
            dimension_semantics=("arbitrary", "arbitrary"), vmem_limit_bytes=VMEM_LIMIT),
        name="in_proj",
    )(x, norm1, wt, wr, gmat, cos_t, sin_t, cos_k, sin_k, qg, kg, kgs, mqg, wpool, pscale, mk, mv_t)


def _absmax(ref):
    return lax.fori_loop(0, ref.shape[0], lambda i, m: jnp.maximum(m, jnp.abs(ref[i])), jnp.float32(0.0))


def _diff_attn_kernel(qn_ref, kn_ref, q_t_ref, k_ref, v_t_ref, lq1_ref, lk1_ref, lq2_ref, lk2_ref, subg_ref, dmask_ref,
                      o_ref, rhs_ref, m_ref, l_ref, acc_ref, sa_ref, sb_ref):
    qi = pl.program_id(2)
    tq = q_t_ref.shape[2]
    tk = tq
    q_t = q_t_ref[0]
    row = lax.broadcasted_iota(jnp.int32, q_t.shape, 0)
    zero = jnp.zeros_like(q_t)
    rhs_ref[0] = jnp.where(row < HEAD_DIM, q_t, zero)
    rhs_ref[1] = jnp.where(row >= HEAD_DIM, q_t, zero)
    l_ref[...] = jnp.zeros(l_ref.shape, jnp.float32)
    acc_ref[...] = jnp.zeros(acc_ref.shape, jnp.float32)

    bound = (HEAD_DIM ** 0.5) * LOG2E * _absmax(qn_ref) * _absmax(kn_ref)

    def blocks(j):
        k_blk = k_ref[0, pl.ds(pl.multiple_of(j * tk, tk), tk), :]
        v_blk = v_t_ref[0, 0, j]
        return k_blk, v_blk

    def chunk_mask():
        kc = lax.broadcasted_iota(jnp.int32, (tk, tq), 0) // CHUNK
        qc = lax.broadcasted_iota(jnp.int32, (tk, tq), 1) // CHUNK
        return kc <= qc

    def sweep(step):
        lax.fori_loop(0, qi, lambda j, c: (step(j, False), c)[1], 0)
        step(qi, True)

    @pl.when(bound <= FIXED_OFFSET_LIMIT)
    def _():
        def produce(j, s_ref):
            k_blk = k_ref[0, pl.ds(pl.multiple_of(j * tk, tk), tk), :]
            for c in range(2):
                s_ref[c] = _dot(k_blk, rhs_ref[c])

        def consume(j, s_ref, masked):
            v_blk = v_t_ref[0, 0, j]
            for c in range(2):
                p = jnp.exp2(s_ref[c] - bound)
                if masked:
                    p = p * dmask_ref[...]
                l_ref[c] += jnp.sum(p.reshape(tk // 8, 8, tq), axis=0)
                acc_ref[c] += _dot(v_blk, p.astype(jnp.bfloat16))

        produce(0, sa_ref)

        def pair(i, carry):
            produce(2 * i + 1, sb_ref)
            consume(2 * i, sa_ref, False)
            produce(2 * i + 2, sa_ref)
            consume(2 * i + 1, sb_ref, False)
            return carry

        lax.fori_loop(0, qi // 2, pair, 0)

        @pl.when(qi % 2 == 0)
        def _():
            consume(qi, sa_ref, True)

        @pl.when(qi % 2 == 1)
        def _():
            produce(qi, sb_ref)
            consume(qi - 1, sa_ref, False)
            consume(qi, sb_ref, True)

    @pl.when(bound > FIXED_OFFSET_LIMIT)
    def _():
        m_ref[...] = jnp.full(m_ref.shape, NEG_INF, jnp.float32)

        def step(j, masked):
            k_blk, v_blk = blocks(j)
            mask = chunk_mask() if masked else None
            for c in range(2):
                s = _dot(k_blk, rhs_ref[c])
                if masked:
                    s = jnp.where(mask, s, NEG_INF)
                m_prev = m_ref[c]
                m_new = jnp.maximum(m_prev, jnp.max(s, axis=0, keepdims=True))
                alpha = jnp.exp2(m_prev - m_new)
                p = jnp.exp2(s - m_new)
                l_ref[c, 0:1] = alpha * l_ref[c, 0:1] + jnp.sum(p, axis=0, keepdims=True)
                acc_ref[c] = alpha * acc_ref[c] + _dot(v_blk, p.astype(jnp.bfloat16))
                m_ref[c] = m_new
        sweep(step)

    lam = (jnp.exp(jnp.sum(lq1_ref[...] * lk1_ref[...], axis=-1, keepdims=True))
           - jnp.exp(jnp.sum(lq2_ref[...] * lk2_ref[...], axis=-1, keepdims=True)) + LAM_INIT)
    l0 = jnp.sum(l_ref[0], axis=0, keepdims=True)
    l1 = jnp.sum(l_ref[1], axis=0, keepdims=True)
    o = acc_ref[0] / l0 - lam * (acc_ref[1] / l1)
    y = _rms_cols(o, subg_ref[...]) * (1.0 - LAM_INIT)
    o_ref[0] = y.T.astype(jnp.bfloat16)


def _diff_attn(q_norm, k_norm, q_t, k, v_t, lq1, lk1, lq2, lk2, subg_col):
    B, qw, S = q_t.shape
    tq = TOKEN_TILE
    nq = S // tq
    vd = 2 * HEAD_DIM
    lam_spec = pl.BlockSpec((1, HEAD_DIM), lambda b, h, i: (0, 0))
    smem = pl.BlockSpec(memory_space=pltpu.SMEM)
    chunk = jnp.arange(tq) // CHUNK
    diag_mask = (chunk[:, None] <= chunk[None, :]).astype(jnp.float32)
    return pl.pallas_call(
        _diff_attn_kernel,
        grid=(B, DIFF_HEADS, nq),
        in_specs=[
            smem, smem,
            pl.BlockSpec((1, vd, tq), lambda b, h, i: (b, h, i)),
            pl.BlockSpec((1, S, vd), lambda b, h, i: (b, 0, h)),
            pl.BlockSpec((1, 1, nq, vd, tq), lambda b, h, i: (b, h, 0, 0, 0)),
            lam_spec, lam_spec, lam_spec, lam_spec,
            pl.BlockSpec((vd, 1), lambda b, h, i: (0, 0)),
            pl.BlockSpec((tq, tq), lambda b, h, i: (0, 0)),
        ],
        out_specs=pl.BlockSpec((1, tq, vd), lambda b, h, i: (b, i, h)),
        out_shape=jax.ShapeDtypeStruct((B, S, DIFF_HEADS * vd), jnp.bfloat16),
        scratch_shapes=[
            pltpu.VMEM((2, vd, tq), jnp.bfloat16),
            pltpu.VMEM((2, 1, tq), jnp.float32),
            pltpu.VMEM((2, 8, tq), jnp.float32),
            pltpu.VMEM((2, vd, tq), jnp.float32),
            pltpu.VMEM((2, tq, tq), jnp.float32),
            pltpu.VMEM((2, tq, tq), jnp.float32),
        ],
        compiler_params=pltpu.CompilerParams(
            dimension_semantics=("arbitrary", "arbitrary", "arbitrary"), vmem_limit_bytes=VMEM_LIMIT),
        name="diff_attn",
    )(q_norm, k_norm, q_t, k, v_t, lq1, lk1, lq2, lk2, subg_col, diag_mask)


def _merge_kernel(x_ref, norm1_ref, wg_ref, bg_ref, ya_ref, yb_ref, yc_ref, wa_ref, wb_ref, wc_ref, wo_ref,
                  norm2_ref, wr_hi_ref, wr_lo_ref, br_ref, x1_ref, h2_ref, idx_ref, gw_ref):
    D = x_ref.shape[2]
    x = x_ref[0]
    h = _rms_rows(x, norm1_ref[...]).astype(jnp.bfloat16)
    merged = None
    for i, (y_ref, w_ref) in enumerate(((ya_ref, wa_ref), (yb_ref, wb_ref), (yc_ref, wc_ref))):
        gz = _dot(h, wg_ref[:, i * D:(i + 1) * D]) + bg_ref[:, i * D:(i + 1) * D]
        gate = 1.0 / (1.0 + jnp.exp(-gz))
        term = gate * _dot(y_ref[0], w_ref[...])
        merged = term if merged is None else merged + term
    x1 = x + _dot(merged.astype(jnp.bfloat16), wo_ref[...])
    x1_ref[0] = x1
    h2 = _rms_rows(x1, norm2_ref[...])
    _store_token_tiles(h2_ref, h2)

    hi = h2.astype(jnp.bfloat16)
    lo = (h2 - hi.astype(jnp.float32)).astype(jnp.bfloat16)
    logits = (_dot_nt(wr_hi_ref[...], hi) + _dot_nt(wr_hi_ref[...], lo) + _dot_nt(wr_lo_ref[...], hi)
              + br_ref[...])
    eidx = lax.broadcasted_iota(jnp.int32, logits.shape, 0)
    vals = logits
    top_v, top_i = [], []
    for _ in range(TOP_K):
        mx = jnp.max(vals, axis=0, keepdims=True)
        ix = jnp.min(jnp.where(vals == mx, eidx, N_EXPERTS), axis=0, keepdims=True)
        top_v.append(mx)
        top_i.append(ix)
        vals = jnp.where(eidx == ix, -jnp.inf, vals)
    ex = [jnp.exp(v - top_v[0]) for v in top_v]
    den = ex[0] + ex[1] + ex[2] + ex[3]
    for k in range(TOP_K):
        idx_ref[k, 0] = top_i[k]
        gw_ref[k, 0] = ex[k] / den


def _merge(x, norm1, wg, bg, ya, yb, yc, wa, wb, wc, wo, norm2, wr_hi, wr_lo, br_col):
    B, S, D = x.shape
    tm = TOKEN_TILE
    const = lambda b, s: (0, 0)
    tok = lambda b, s: (b, s, 0)
    full = lambda a: pl.BlockSpec(a.shape, const)
    return pl.pallas_call(
        _merge_kernel,
        grid=(B, S // tm),
        in_specs=[
            pl.BlockSpec((1, tm, D), tok), full(norm1), full(wg), full(bg),
            pl.BlockSpec((1, tm, ya.shape[2]), tok), pl.BlockSpec((1, tm, yb.shape[2]), tok),
            pl.BlockSpec((1, tm, yc.shape[2]), tok),
            full(wa), full(wb), full(wc), full(wo), full(norm2), full(wr_hi), full(wr_lo), full(br_col),
        ],
        out_specs=[
            pl.BlockSpec((1, tm, D), tok),
            pl.BlockSpec((tm * SUBLANES, LANES), lambda b, s: (b * (S // tm) + s, 0)),
            pl.BlockSpec((TOP_K, 1, 1, tm), lambda b, s: (0, b * (S // tm) + s, 0, 0)),
            pl.BlockSpec((TOP_K, 1, 1, tm), lambda b, s: (0, b * (S // tm) + s, 0, 0)),
        ],
        out_shape=[
            jax.ShapeDtypeStruct((B, S, D), jnp.float32),
            jax.ShapeDtypeStruct((B * S * SUBLANES, LANES), jnp.float32),
            jax.ShapeDtypeStruct((TOP_K, B * S // tm, 1, tm), jnp.int32),
            jax.ShapeDtypeStruct((TOP_K, B * S // tm, 1, tm), jnp.float32),
        ],
        compiler_params=pltpu.CompilerParams(
            dimension_semantics=("arbitrary", "arbitrary"), vmem_limit_bytes=VMEM_LIMIT),
        name="merge",
    )(x, norm1, wg, bg, ya, yb, yc, wa, wb, wc, wo, norm2, wr_hi, wr_lo, br_col)


def _experts_kernel(nused_ref, bexp_ref, tok_cur_ref, tok_nxt_ref, dst_ref, h2_hbm, roww_ref, eye_ref, wgu_ref,
                    bgu_ref, wd_ref, bd_ref, y4_hbm, xbuf, ybuf, wgu_bf, wd_bf, gsem, ssem):
    b = pl.program_id(0)
    nb = pl.num_programs(0)
    nused = nused_ref[0]
    rows = xbuf.shape[1] // SUBLANES
    F = wd_ref.shape[1]
    slot = b % 2

    def tile(ref, off):
        return ref.at[pl.ds(pl.multiple_of(off, SUBLANES), SUBLANES)]

    def gather(tok_ref, buf_slot):
        for i in range(rows):
            pltpu.make_async_copy(tile(h2_hbm, tok_ref[0, 0, i]), xbuf.at[buf_slot, pl.ds(i * SUBLANES, SUBLANES)],
                                  gsem.at[buf_slot]).start()

    def wait_rows(buf, sem):
        pltpu.make_async_copy(buf, buf, sem).wait()

    @pl.when(b == 0)
    def _():
        gather(tok_cur_ref, 0)
        ybuf[...] = jnp.zeros_like(ybuf)
        n_real = y4_hbm.shape[0] - 2 * rows * SUBLANES
        for s in range(2):
            spare = pltpu.make_async_copy(
                ybuf.at[s], y4_hbm.at[pl.ds(n_real + s * rows * SUBLANES, rows * SUBLANES)], ssem.at[s])
            spare.start()
            spare.wait()

    @pl.when(jnp.logical_and(b >= 2, b - 2 < nused))
    def _():
        wait_rows(ybuf.at[slot], ssem.at[slot])

    changed = jnp.logical_or(b == 0, bexp_ref[b] != bexp_ref[jnp.maximum(b - 1, 0)])

    @pl.when(jnp.logical_and(changed, b < nused))
    def _():
        chunk = 128
        def cast(i, carry):
            r = pl.ds(pl.multiple_of(i * chunk, chunk), chunk)
            wgu_bf[r, :] = wgu_ref[0, r, :].astype(jnp.bfloat16)
            wd_bf[r, :] = wd_ref[0, r, :].astype(jnp.bfloat16)
            return carry
        lax.fori_loop(0, wgu_bf.shape[0] // chunk, cast, 0)

    @pl.when(jnp.logical_or(b == 0, b <= nused))
    def _():
        wait_rows(xbuf.at[slot], gsem.at[slot])

    @pl.when(b < nused)
    def _():
        gather(tok_nxt_ref, 1 - slot)
        x = _load_token_tiles(xbuf.at[slot], rows).astype(jnp.bfloat16)
        gu = _dot(x, wgu_bf[...]) + bgu_ref[0]
        gate = jnp.minimum(gu[:, :F], SWIGLU_LIMIT)
        up = jnp.clip(gu[:, F:], -SWIGLU_LIMIT, SWIGLU_LIMIT)
        act = gate * (1.0 / (1.0 + jnp.exp(-SWIGLU_ALPHA * gate))) * (up + 1.0)
        y = _dot(act.astype(jnp.bfloat16), wd_bf[...]) + bd_ref[0]
        w_row = roww_ref[0]
        w_hi = w_row.astype(jnp.bfloat16)
        w_lo = (w_row - w_hi.astype(jnp.float32)).astype(jnp.bfloat16)
        ones = jnp.ones((rows, LANES), jnp.bfloat16)
        w_col = _dot(eye_ref[...] * w_hi, ones) + _dot(eye_ref[...] * w_lo, ones)
        _store_token_tiles(ybuf.at[slot], y * jnp.concatenate([w_col] * (y.shape[1] // LANES), axis=1))
        for i in range(rows):
            pltpu.make_async_copy(ybuf.at[slot, pl.ds(i * SUBLANES, SUBLANES)], tile(y4_hbm, dst_ref[0, 0, i]),
                                  ssem.at[slot]).start()

    @pl.when(b == nb - 1)
    def _():
        @pl.when(jnp.logical_and(b >= 1, b - 1 < nused))
        def _():
            wait_rows(ybuf.at[1 - slot], ssem.at[1 - slot])

        @pl.when(b < nused)
        def _():
            wait_rows(ybuf.at[slot], ssem.at[slot])
            wait_rows(xbuf.at[1 - slot], gsem.at[1 - slot])


def _experts(nused, blk_exp, row_tok, row_dst, h2, row_w, wgu, bgu, wd, bd, n_out_rows):
    D = wgu.shape[1]
    nb = row_tok.shape[0]
    rows = row_tok.shape[2]
    F = wd.shape[1]
    grid_spec = pltpu.PrefetchScalarGridSpec(
        num_scalar_prefetch=2,
        grid=(nb,),
        in_specs=[
            pl.BlockSpec((1, 1, rows), lambda b, nu, be: (b, 0, 0), memory_space=pltpu.SMEM),
            pl.BlockSpec((1, 1, rows), lambda b, nu, be: (jnp.minimum(b + 1, nb - 1), 0, 0), memory_space=pltpu.SMEM),
            pl.BlockSpec((1, 1, rows), lambda b, nu, be: (b, 0, 0), memory_space=pltpu.SMEM),
            pl.BlockSpec(memory_space=pl.ANY),
            pl.BlockSpec((1, 1, rows), lambda b, nu, be: (b, 0, 0)),
            pl.BlockSpec((rows, rows), lambda b, nu, be: (0, 0)),
            pl.BlockSpec((1, D, 2 * F), lambda b, nu, be: (be[b], 0, 0)),
            pl.BlockSpec((1, 1, 2 * F), lambda b, nu, be: (be[b], 0, 0)),
            pl.BlockSpec((1, F, D), lambda b, nu, be: (be[b], 0, 0)),
            pl.BlockSpec((1, 1, D), lambda b, nu, be: (be[b], 0, 0)),
        ],
        out_specs=pl.BlockSpec(memory_space=pl.ANY),
        scratch_shapes=[
            pltpu.VMEM((2, rows * SUBLANES, LANES), jnp.float32),
            pltpu.VMEM((2, rows * SUBLANES, LANES), jnp.float32),
            pltpu.VMEM((D, 2 * F), jnp.bfloat16),
            pltpu.VMEM((F, D), jnp.bfloat16),
            pltpu.SemaphoreType.DMA((2,)),
            pltpu.SemaphoreType.DMA((2,)),
        ],
    )
    return pl.pallas_call(
        _experts_kernel,
        grid_spec=grid_spec,
        out_shape=jax.ShapeDtypeStruct((n_out_rows * SUBLANES, LANES), jnp.float32),
        compiler_params=pltpu.CompilerParams(
            dimension_semantics=("arbitrary",), vmem_limit_bytes=VMEM_LIMIT),
        name="experts",
    )(nused, blk_exp, row_tok, row_tok, row_dst, h2, row_w, jnp.eye(rows, dtype=jnp.bfloat16), wgu, bgu, wd, bd)


def _combine_kernel(x1_ref, y0_ref, y1_ref, y2_ref, y3_ref, o_ref):
    tm = x1_ref.shape[0]
    for c in range(SUBLANES):
        rows = pl.ds(c, tm, stride=SUBLANES)
        cols = slice(c * LANES, (c + 1) * LANES)
        o_ref[:, cols] = x1_ref[:, cols] + (((y0_ref[rows, :] + y1_ref[rows, :]) + y2_ref[rows, :]) + y3_ref[rows, :])


def _combine(x1, y4):
    T, D = x1.shape
    tm = TOKEN_TILE
    nt = T // tm
    return pl.pallas_call(
        _combine_kernel,
        grid=(nt,),
        in_specs=[pl.BlockSpec((tm, D), lambda i: (i, 0))]
        + [pl.BlockSpec((tm * SUBLANES, LANES), functools.partial(lambda i, k: (k * nt + i, 0), k=k))
           for k in range(TOP_K)],
        out_specs=pl.BlockSpec((tm, D), lambda i: (i, 0)),
        out_shape=jax.ShapeDtypeStruct((T, D), jnp.float32),
        name="combine",
    )(x1, y4, y4, y4, y4)


def _rope_tables(seq):
    inv_freq = 1.0 / (ROPE_THETA ** (jnp.arange(0, HEAD_DIM, 2, dtype=jnp.float32) / HEAD_DIM))
    ang = jnp.arange(seq, dtype=jnp.float32)[:, None] * inv_freq[None, :]
    return jnp.cos(ang), jnp.sin(ang)


def _dispatch(top_idx, gate_w, rows):
    T = top_idx.shape[1]
    A = T * TOP_K
    e_flat = top_idx.reshape(A)
    order = jnp.argsort(e_flat).astype(jnp.int32)
    experts = jnp.arange(N_EXPERTS, dtype=jnp.int32)
    counts = jnp.sum((e_flat[:, None] == experts[None, :]).astype(jnp.int32), axis=0)
    padded = ((counts + rows - 1) // rows) * rows
    pend = jnp.cumsum(padded)
    pstart = pend - padded
    ustart = jnp.cumsum(counts) - counts
    P = A + N_EXPERTS * rows
    nb = P // rows
    blk_start = jnp.arange(nb, dtype=jnp.int32) * rows
    blk_exp = jnp.minimum(jnp.sum((pend[None, :] <= blk_start[:, None]).astype(jnp.int32), axis=1), N_EXPERTS - 1)
    in_blk = jnp.arange(rows, dtype=jnp.int32)[None, :]
    j = blk_start[:, None] + in_blk - pstart[blk_exp][:, None]
    valid = j < counts[blk_exp][:, None]
    a_flat = jnp.take(order, jnp.clip(ustart[blk_exp][:, None] + j, 0, A - 1).reshape(P))
    w_flat = jnp.take(gate_w.reshape(A), a_flat).reshape(nb, rows)
    a = a_flat.reshape(nb, rows)
    row_tok = jnp.where(valid, a % T, 0)
    row_dst = jnp.where(valid, a, A + (jnp.arange(nb, dtype=jnp.int32)[:, None] % 2) * rows + in_blk)
    row_w = jnp.where(valid, w_flat, 0.0)
    nused = (pend[-1] // rows).astype(jnp.int32).reshape(1)
    return (nused, blk_exp, (row_tok * SUBLANES).reshape(nb, 1, rows), (row_dst * SUBLANES).reshape(nb, 1, rows),
            row_w.reshape(nb, 1, rows), A + 2 * rows)


def kernel(x, mem, norm1, w_in, b_gate, q_norm, k_norm, lambda_q1, lambda_k1, lambda_q2, lambda_k2, diff_subln,
           w_pool, pool_scale, mem_norm, w_mem_kv, mq_norm, mk_norm, w_br_diff, w_br_pool, w_br_mem, w_out, norm2,
           w_router, b_router, w_gate_up, b_gate_up, w_down, b_down):
    B, S, D = x.shape
    depth = norm1.shape[0]
    bf = jnp.bfloat16
    qw = 2 * DIFF_HEADS * HEAD_DIM
    vw = DIFF_HEADS * 2 * HEAD_DIM
    pw = w_pool.shape[1] * w_pool.shape[2]
    mw = MEM_HEADS * HEAD_DIM
    k_off, v_off, pool_off, mq_off, gate_off = qw, 2 * qw, 2 * qw + vw, 2 * qw + vw + pw, 2 * qw + vw + pw + mw
    half = HEAD_DIM // 2

    hpd = (jnp.arange(2)[None, :, None] * (DIFF_HEADS * HEAD_DIM) + jnp.arange(DIFF_HEADS)[:, None, None] * HEAD_DIM
           + jnp.arange(HEAD_DIM)[None, None, :])
    perm = hpd.reshape(-1)
    perm_swapped = (hpd - hpd % HEAD_DIM + (hpd % HEAD_DIM + half) % HEAD_DIM).reshape(-1)
    cos, sin = _rope_tables(S)
    cos_t, sin_t = cos.T, sin.T
    cos_k = jnp.tile(cos, (1, 4))
    sin_k = jnp.tile(jnp.concatenate([-sin, sin], axis=1), (1, 2))
    gidx = jnp.arange(qw) // HEAD_DIM
    gmat = (gidx[:, None] == gidx[None, :]).astype(bf)
    pidx = jnp.arange(pw) // w_pool.shape[2]

    for l in range(depth):
        wi = w_in[l]
        wt = jnp.concatenate([wi[:, perm], wi[:, v_off:pool_off], wi[:, mq_off:gate_off]], axis=1).T.astype(bf)
        wr = jnp.concatenate([wi[:, k_off + perm], wi[:, k_off + perm_swapped], wi[:, pool_off:mq_off]],
                             axis=1).astype(bf)
        wg = wi[:, gate_off:].astype(bf)
        kg = jnp.tile(k_norm[l], 2)[None, :]
        kgs = jnp.tile(jnp.roll(k_norm[l], half), 2)[None, :]
        wpool_bd = jnp.where(pidx[:, None] == pidx[None, :],
                             jnp.tile(w_pool[l].reshape(pw, -1), (1, w_pool.shape[1])), 0.0).astype(bf)
        mk, mv_t = _mem_kv(mem, mem_norm[l][None, :], w_mem_kv[l].T.astype(bf), mk_norm[l][:, None])
        q_t, k, v_t, y_b, y_c = _in_proj(
            x, norm1[l][None, :], wt, wr, gmat, cos_t, sin_t, cos_k, sin_k, q_norm[l][:, None], kg, kgs,
            mq_norm[l][:, None], wpool_bd, pool_scale[l][None, :], mk, mv_t)
        y_a = _diff_attn(q_norm[l], k_norm[l], q_t, k, v_t, lambda_q1[l][None, :], lambda_k1[l][None, :], lambda_q2[l][None, :],
                         lambda_k2[l][None, :], diff_subln[l][:, None])
        wr_t = w_router[l].T
        wr_hi = wr_t.astype(bf)
        wr_lo = (wr_t - wr_hi.astype(jnp.float32)).astype(bf)
        x1, h2, top_idx, gate_w = _merge(
            x, norm1[l][None, :], wg, b_gate[l][None, :], y_a, y_b, y_c, w_br_diff[l].astype(bf),
            w_br_pool[l].astype(bf), w_br_mem[l].astype(bf), w_out[l].astype(bf), norm2[l][None, :], wr_hi, wr_lo,
            b_router[l][:, None])
        T = B * S
        top_idx = top_idx.reshape(TOP_K, T)
        gate_w = gate_w.reshape(TOP_K, T)
        nused, blk_exp, row_tok, row_dst, row_w, P = _dispatch(top_idx, gate_w, EXPERT_ROWS)
        y4 = _experts(nused, blk_exp, row_tok, row_dst, h2, row_w, w_gate_up[l],
                      b_gate_up[l][:, None, :], w_down[l], b_down[l][:, None, :], P)
        x = _combine(x1.reshape(T, D), y4).reshape(B, S, D)
    return x
```

```python
import functools
import math

import jax
import jax.numpy as jnp
from jax import lax
from jax.experimental import pallas as pl
from jax.experimental.pallas import tpu as pltpu

HEAD_DIM = 64
CHUNK = 64
RMS_EPS = 1e-6
ROPE_THETA = 10000.0
DIFF_HEADS = 4
MEM_HEADS = 4
POOL_WINDOWS = (2, 4, 8, 16)
POOL_HALO = 16
N_EXPERTS = 32
TOP_K = 4
SWIGLU_LIMIT = 7.0
SWIGLU_ALPHA = 1.702
NEG_INF = -1e30
LAM_INIT = 0.8 - 0.6 * math.exp(-0.3 * 0)
LOG2E = math.log2(math.e)
FIXED_OFFSET_LIMIT = 40.0 * LOG2E

SUBLANES, LANES = 8, 128
TOKEN_TILE = 512
EXPERT_ROWS = 256
VMEM_LIMIT = 56 * 1024 * 1024

_NT = (((1,), (1,)), ((), ()))


def _dot(a, b):
    return jnp.dot(a, b, preferred_element_type=jnp.float32)


def _dot_nt(a, b):
    return lax.dot_general(a, b, _NT, preferred_element_type=jnp.float32)


def _store_token_tiles(ref, x):
    n = x.shape[0]
    for c in range(SUBLANES):
        ref[pl.ds(c, n, stride=SUBLANES), :] = x[:, c * LANES:(c + 1) * LANES]


def _load_token_tiles(ref, n):
    return jnp.concatenate([ref[pl.ds(c, n, stride=SUBLANES), :] for c in range(SUBLANES)], axis=1)


def _rms_rows(x, gain_row):
    return x * lax.rsqrt(jnp.mean(x * x, axis=-1, keepdims=True) + RMS_EPS) * gain_row


def _rms_cols(x, gain_col):
    return x * lax.rsqrt(jnp.mean(x * x, axis=0, keepdims=True) + RMS_EPS) * gain_col


def _mem_kv_kernel(mem_ref, gain_ref, wkv_t_ref, mk_gain_ref, mk_ref, mv_t_ref):
    mem_n = _rms_rows(mem_ref[0], gain_ref[...]).astype(jnp.bfloat16)
    kv_t = _dot_nt(wkv_t_ref[...], mem_n)
    mw = kv_t.shape[0] // 2
    for h in range(MEM_HEADS):
        blk = kv_t[h * HEAD_DIM:(h + 1) * HEAD_DIM]
        mk_t = _rms_cols(blk, mk_gain_ref[...]) * (HEAD_DIM ** -0.5)
        mk_ref[0, h] = mk_t.T.astype(jnp.bfloat16)
    mv_t_ref[0] = kv_t[mw:].astype(jnp.bfloat16)


def _mem_kv(mem, mem_norm, wkv_t, mk_gain_col):
    B, M, D = mem.shape
    mw = wkv_t.shape[0] // 2
    return pl.pallas_call(
        _mem_kv_kernel,
        grid=(B,),
        in_specs=[
            pl.BlockSpec((1, M, D), lambda b: (b, 0, 0)),
            pl.BlockSpec((1, D), lambda b: (0, 0)),
            pl.BlockSpec((2 * mw, D), lambda b: (0, 0)),
            pl.BlockSpec((HEAD_DIM, 1), lambda b: (0, 0)),
        ],
        out_specs=[
            pl.BlockSpec((1, MEM_HEADS, M, HEAD_DIM), lambda b: (b, 0, 0, 0)),
            pl.BlockSpec((1, mw, M), lambda b: (b, 0, 0)),
        ],
        out_shape=[
            jax.ShapeDtypeStruct((B, MEM_HEADS, M, HEAD_DIM), jnp.bfloat16),
            jax.ShapeDtypeStruct((B, mw, M), jnp.bfloat16),
        ],
        name="mem_kv",
    )(mem, mem_norm, wkv_t, mk_gain_col)


def _in_proj_kernel(x_ref, norm1_ref, wt_ref, wr_ref, gmat_ref, cos_t_ref, sin_t_ref, cos_k_ref, sin_k_ref,
                    qg_ref, kg_ref, kgs_ref, mqg_ref, wpool_ref, pscale_ref, mk_ref, mv_t_ref,
                    q_t_ref, k_ref, v_t_ref, yb_ref, yc_ref, halo_ref, ext_ref):
    s_idx = pl.program_id(1)
    tm = x_ref.shape[1]
    h = _rms_rows(x_ref[0], norm1_ref[...]).astype(jnp.bfloat16)
    z_t = _dot_nt(wt_ref[...], h)
    z_r = _dot(h, wr_ref[...])

    cos_t, sin_t = cos_t_ref[...], sin_t_ref[...]
    half = HEAD_DIM // 2
    for g in range(2 * DIFF_HEADS):
        y = _rms_cols(z_t[g * HEAD_DIM:(g + 1) * HEAD_DIM], qg_ref[...]) * (HEAD_DIM ** -0.5 * LOG2E)
        t1, t2 = y[:half], y[half:]
        q_t_ref[0, g * HEAD_DIM:g * HEAD_DIM + half] = (t1 * cos_t - t2 * sin_t).astype(jnp.bfloat16)
        q_t_ref[0, g * HEAD_DIM + half:(g + 1) * HEAD_DIM] = (t2 * cos_t + t1 * sin_t).astype(jnp.bfloat16)

    qw = 2 * DIFF_HEADS * HEAD_DIM
    vd = 2 * HEAD_DIM
    for hd in range(DIFF_HEADS):
        v_t_ref[0, hd, 0] = z_t[qw + hd * vd:qw + (hd + 1) * vd].astype(jnp.bfloat16)

    mq_off = qw + DIFF_HEADS * vd
    outs = []
    for hd in range(MEM_HEADS):
        mq_t = _rms_cols(z_t[mq_off + hd * HEAD_DIM:mq_off + (hd + 1) * HEAD_DIM], mqg_ref[...])
        s_t = _dot(mk_ref[0, hd], mq_t.astype(jnp.bfloat16))
        p_t = jnp.exp(s_t - jnp.max(s_t, axis=0, keepdims=True))
        o_t = _dot(mv_t_ref[0, hd * HEAD_DIM:(hd + 1) * HEAD_DIM], p_t.astype(jnp.bfloat16))
        outs.append(o_t / jnp.sum(p_t, axis=0, keepdims=True))
    yc_ref[0] = jnp.concatenate(outs, axis=0).T.astype(jnp.bfloat16)

    kw = 2 * DIFF_HEADS * HEAD_DIM
    zk, zks = z_r[:, :kw], z_r[:, kw:2 * kw]
    ssq = _dot((zk * zk).astype(jnp.bfloat16), gmat_ref[...])
    r = lax.rsqrt(ssq * (1.0 / HEAD_DIM) + RMS_EPS)
    cos_k, sin_k = cos_k_ref[...], sin_k_ref[...]
    for j in range(kw // 128):
        sl = slice(j * 128, (j + 1) * 128)
        kk = r[:, sl] * (zk[:, sl] * kg_ref[...] * cos_k + zks[:, sl] * kgs_ref[...] * sin_k)
        k_ref[0, :, sl] = kk.astype(jnp.bfloat16)

    u = z_r[:, 2 * kw:]

    @pl.when(s_idx == 0)
    def _():
        halo_ref[...] = jnp.zeros_like(halo_ref)

    ext_ref[0:POOL_HALO] = halo_ref[...]
    ext_ref[POOL_HALO:] = u
    halo_ref[...] = u[tm - POOL_HALO:]
    pos1 = (s_idx * tm + 1 + lax.broadcasted_iota(jnp.int32, (tm, 128), 0)).astype(jnp.float32)
    lane = lax.broadcasted_iota(jnp.int32, (tm, 128), 1)
    pooled = []
    for part in range(2):
        cols = slice(part * 128, (part + 1) * 128)
        w_a, w_b = POOL_WINDOWS[2 * part], POOL_WINDOWS[2 * part + 1]
        acc = ext_ref[POOL_HALO:POOL_HALO + tm, cols]
        sum_a = None
        for j in range(1, w_b):
            if j == w_a:
                sum_a = acc
            acc = acc + ext_ref[POOL_HALO - j:POOL_HALO - j + tm, cols]
        mean_a = sum_a / jnp.minimum(pos1, float(w_a))
        mean_b = acc / jnp.minimum(pos1, float(w_b))
        pooled.append(jnp.where(lane < 64, mean_a, mean_b) - u[:, cols])
    pooled = jnp.concatenate(pooled, axis=1).astype(jnp.bfloat16)
    yb_ref[0] = (_dot(pooled, wpool_ref[...]) * pscale_ref[...]).astype(jnp.bfloat16)


def _in_proj(x, norm1, wt, wr, gmat, cos_t, sin_t, cos_k, sin_k, qg, kg, kgs, mqg, wpool, pscale, mk, mv_t):
    B, S, D = x.shape
    tm = TOKEN_TILE
    ns = S // tm
    nt, nr = wt.shape[0], wr.shape[1]
    qw = 2 * DIFF_HEADS * HEAD_DIM
    vd = 2 * HEAD_DIM
    pw = wpool.shape[0]
    M = mk.shape[2]
    mw = mv_t.shape[1]
    const = lambda b, s: (0, 0)
    return pl.pallas_call(
        _in_proj_kernel,
        grid=(B, ns),
        in_specs=[
            pl.BlockSpec((1, tm, D), lambda b, s: (b, s, 0)),
            pl.BlockSpec((1, D), const),
            pl.BlockSpec((nt, D), const),
            pl.BlockSpec((D, nr), const),
            pl.BlockSpec((qw, qw), const),
            pl.BlockSpec((HEAD_DIM // 2, tm), lambda b, s: (0, s)),
            pl.BlockSpec((HEAD_DIM // 2, tm), lambda b, s: (0, s)),
            pl.BlockSpec((tm, 128), lambda b, s: (s, 0)),
            pl.BlockSpec((tm, 128), lambda b, s: (s, 0)),
            pl.BlockSpec((HEAD_DIM, 1), const),
            pl.BlockSpec((1, 128), const),
            pl.BlockSpec((1, 128), const),
            pl.BlockSpec((HEAD_DIM, 1), const),
            pl.BlockSpec((pw, pw), const),
            pl.BlockSpec((1, pw), const),
            pl.BlockSpec((1, MEM_HEADS, M, HEAD_DIM), lambda b, s: (b, 0, 0, 0)),
            pl.BlockSpec((1, mw, M), lambda b, s: (b, 0, 0)),
        ],
        out_specs=[
            pl.BlockSpec((1, qw, tm), lambda b, s: (b, 0, s)),
            pl.BlockSpec((1, tm, qw), lambda b, s: (b, s, 0)),
            pl.BlockSpec((1, DIFF_HEADS, 1, vd, tm), lambda b, s: (b, 0, s, 0, 0)),
            pl.BlockSpec((1, tm, pw), lambda b, s: (b, s, 0)),
            pl.BlockSpec((1, tm, mw), lambda b, s: (b, s, 0)),
        ],
        out_shape=[
            jax.ShapeDtypeStruct((B, qw, S), jnp.bfloat16),
            jax.ShapeDtypeStruct((B, S, qw), jnp.bfloat16),
            jax.ShapeDtypeStruct((B, DIFF_HEADS, ns, vd, tm), jnp.bfloat16),
            jax.ShapeDtypeStruct((B, S, pw), jnp.bfloat16),
            jax.ShapeDtypeStruct((B, S, mw), jnp.bfloat16),
        ],
        scratch_shapes=[
            pltpu.VMEM((POOL_HALO, pw), jnp.float32),
            pltpu.VMEM((POOL_HALO + tm, pw), jnp.float32),
        ],
        compiler_params=pltpu.CompilerParams(
            dimension_semantics=("arbitrary", "arbitrary"), vmem_limit_bytes=VMEM_LIMIT),
        name="in_proj",
    )(x, norm1, wt, wr, gmat, cos_t, sin_t, cos_k, sin_k, qg, kg, kgs, mqg, wpool, pscale, mk, mv_t)


def _absmax(ref):
    return lax.fori_loop(0, ref.shape[0], lambda i, m: jnp.maximum(m, jnp.abs(ref[i])), jnp.float32(0.0))


def _diff_attn_kernel(qn_ref, kn_ref, q_t_ref, k_ref, v_t_ref, lq1_ref, lk1_ref, lq2_ref, lk2_ref, subg_ref, o_ref,
                      rhs_ref, m_ref, l_ref, acc_ref, sa_ref, sb_ref):
    qi = pl.program_id(2)
    tq = q_t_ref.shape[2]
    tk = tq
    q_t = q_t_ref[0]
    row = lax.broadcasted_iota(jnp.int32, q_t.shape, 0)
    zero = jnp.zeros_like(q_t)
    rhs_ref[0] = jnp.where(row < HEAD_DIM, q_t, zero)
    rhs_ref[1] = jnp.where(row >= HEAD_DIM, q_t, zero)
    l_ref[...] = jnp.zeros(l_ref.shape, jnp.float32)
    acc_ref[...] = jnp.zeros(acc_ref.shape, jnp.float32)

    bound = (HEAD_DIM ** 0.5) * LOG2E * _absmax(qn_ref) * _absmax(kn_ref)

    def blocks(j):
        k_blk = k_ref[0, pl.ds(pl.multiple_of(j * tk, tk), tk), :]
        v_blk = v_t_ref[0, 0, j]
        return k_blk, v_blk

    def chunk_mask():
        kc = lax.broadcasted_iota(jnp.int32, (tk, tq), 0) // CHUNK
        qc = lax.broadcasted_iota(jnp.int32, (tk, tq), 1) // CHUNK
        return kc <= qc

    def sweep(step):
        lax.fori_loop(0, qi, lambda j, c: (step(j, False), c)[1], 0)
        step(qi, True)

    @pl.when(bound <= FIXED_OFFSET_LIMIT)
    def _():
        def produce(j, s_ref):
            k_blk = k_ref[0, pl.ds(pl.multiple_of(j * tk, tk), tk), :]
            for c in range(2):
                s_ref[c] = _dot(k_blk, rhs_ref[c])

        def consume(j, s_ref, masked):
            v_blk = v_t_ref[0, 0, j]
            mask = chunk_mask() if masked else None
            for c in range(2):
                p = jnp.exp2(s_ref[c] - bound)
                if masked:
                    p = jnp.where(mask, p, 0.0)
                l_ref[c] += jnp.sum(p.reshape(tk // 8, 8, tq), axis=0)
                acc_ref[c] += _dot(v_blk, p.astype(jnp.bfloat16))

        produce(0, sa_ref)

        def pair(i, carry):
            produce(2 * i + 1, sb_ref)
            consume(2 * i, sa_ref, False)
            produce(2 * i + 2, sa_ref)
            consume(2 * i + 1, sb_ref, False)
            return carry

        lax.fori_loop(0, qi // 2, pair, 0)

        @pl.when(qi % 2 == 0)
        def _():
            consume(qi, sa_ref, True)

        @pl.when(qi % 2 == 1)
        def _():
            produce(qi, sb_ref)
            consume(qi - 1, sa_ref, False)
            consume(qi, sb_ref, True)

    @pl.when(bound > FIXED_OFFSET_LIMIT)
    def _():
        m_ref[...] = jnp.full(m_ref.shape, NEG_INF, jnp.float32)

        def step(j, masked):
            k_blk, v_blk = blocks(j)
            mask = chunk_mask() if masked else None
            for c in range(2):
                s = _dot(k_blk, rhs_ref[c])
                if masked:
                    s = jnp.where(mask, s, NEG_INF)
                m_prev = m_ref[c]
                m_new = jnp.maximum(m_prev, jnp.max(s, axis=0, keepdims=True))
                alpha = jnp.exp2(m_prev - m_new)
                p = jnp.exp2(s - m_new)
                l_ref[c, 0:1] = alpha * l_ref[c, 0:1] + jnp.sum(p, axis=0, keepdims=True)
                acc_ref[c] = alpha * acc_ref[c] + _dot(v_blk, p.astype(jnp.bfloat16))
                m_ref[c] = m_new
        sweep(step)

    lam = (jnp.exp(jnp.sum(lq1_ref[...] * lk1_ref[...], axis=-1, keepdims=True))
           - jnp.exp(jnp.sum(lq2_ref[...] * lk2_ref[...], axis=-1, keepdims=True)) + LAM_INIT)
    l0 = jnp.sum(l_ref[0], axis=0, keepdims=True)
    l1 = jnp.sum(l_ref[1], axis=0, keepdims=True)
    o = acc_ref[0] / l0 - lam * (acc_ref[1] / l1)
    y = _rms_cols(o, subg_ref[...]) * (1.0 - LAM_INIT)
    o_ref[0] = y.T.astype(jnp.bfloat16)


def _diff_attn(q_norm, k_norm, q_t, k, v_t, lq1, lk1, lq2, lk2, subg_col):
    B, qw, S = q_t.shape
    tq = TOKEN_TILE
    nq = S // tq
    vd = 2 * HEAD_DIM
    lam_spec = pl.BlockSpec((1, HEAD_DIM), lambda b, h, i: (0, 0))
    smem = pl.BlockSpec(memory_space=pltpu.SMEM)
    return pl.pallas_call(
        _diff_attn_kernel,
        grid=(B, DIFF_HEADS, nq),
        in_specs=[
            smem, smem,
            pl.BlockSpec((1, vd, tq), lambda b, h, i: (b, h, i)),
            pl.BlockSpec((1, S, vd), lambda b, h, i: (b, 0, h)),
            pl.BlockSpec((1, 1, nq, vd, tq), lambda b, h, i: (b, h, 0, 0, 0)),
            lam_spec, lam_spec, lam_spec, lam_spec,
            pl.BlockSpec((vd, 1), lambda b, h, i: (0, 0)),
        ],
        out_specs=pl.BlockSpec((1, tq, vd), lambda b, h, i: (b, i, h)),
        out_shape=jax.ShapeDtypeStruct((B, S, DIFF_HEADS * vd), jnp.bfloat16),
        scratch_shapes=[
            pltpu.VMEM((2, vd, tq), jnp.bfloat16),
            pltpu.VMEM((2, 1, tq), jnp.float32),
            pltpu.VMEM((2, 8, tq), jnp.float32),
            pltpu.VMEM((2, vd, tq), jnp.float32),
            pltpu.VMEM((2, tq, tq), jnp.float32),
            pltpu.VMEM((2, tq, tq), jnp.float32),
        ],
        compiler_params=pltpu.CompilerParams(
            dimension_semantics=("arbitrary", "arbitrary", "arbitrary"), vmem_limit_bytes=VMEM_LIMIT),
        name="diff_attn",
    )(q_norm, k_norm, q_t, k, v_t, lq1, lk1, lq2, lk2, subg_col)


def _merge_kernel(x_ref, norm1_ref, wg_ref, bg_ref, ya_ref, yb_ref, yc_ref, wa_ref, wb_ref, wc_ref, wo_ref,
                  norm2_ref, wr_hi_ref, wr_lo_ref, br_ref, x1_ref, h2_ref, idx_ref, gw_ref):
    D = x_ref.shape[2]
    x = x_ref[0]
    h = _rms_rows(x, norm1_ref[...]).astype(jnp.bfloat16)
    merged = None
    for i, (y_ref, w_ref) in enumerate(((ya_ref, wa_ref), (yb_ref, wb_ref), (yc_ref, wc_ref))):
        gz = _dot(h, wg_ref[:, i * D:(i + 1) * D]) + bg_ref[:, i * D:(i + 1) * D]
        gate = 1.0 / (1.0 + jnp.exp(-gz))
        term = gate * _dot(y_ref[0], w_ref[...])
        merged = term if merged is None else merged + term
    x1 = x + _dot(merged.astype(jnp.bfloat16), wo_ref[...])
    x1_ref[0] = x1
    h2 = _rms_rows(x1, norm2_ref[...])
    _store_token_tiles(h2_ref, h2)

    hi = h2.astype(jnp.bfloat16)
    lo = (h2 - hi.astype(jnp.float32)).astype(jnp.bfloat16)
    logits = (_dot_nt(wr_hi_ref[...], hi) + _dot_nt(wr_hi_ref[...], lo) + _dot_nt(wr_lo_ref[...], hi)
              + br_ref[...])
    eidx = lax.broadcasted_iota(jnp.int32, logits.shape, 0)
    vals = logits
    top_v, top_i = [], []
    for _ in range(TOP_K):
        mx = jnp.max(vals, axis=0, keepdims=True)
        ix = jnp.min(jnp.where(vals == mx, eidx, N_EXPERTS), axis=0, keepdims=True)
        top_v.append(mx)
        top_i.append(ix)
        vals = jnp.where(eidx == ix, -jnp.inf, vals)
    ex = [jnp.exp(v - top_v[0]) for v in top_v]
    den = ex[0] + ex[1] + ex[2] + ex[3]
    for k in range(TOP_K):
        idx_ref[k, 0] = top_i[k]
        gw_ref[k, 0] = ex[k] / den


def _merge(x, norm1, wg, bg, ya, yb, yc, wa, wb, wc, wo, norm2, wr_hi, wr_lo, br_col):
    B, S, D = x.shape
    tm = TOKEN_TILE
    const = lambda b, s: (0, 0)
    tok = lambda b, s: (b, s, 0)
    full = lambda a: pl.BlockSpec(a.shape, const)
    return pl.pallas_call(
        _merge_kernel,
        grid=(B, S // tm),
        in_specs=[
            pl.BlockSpec((1, tm, D), tok), full(norm1), full(wg), full(bg),
            pl.BlockSpec((1, tm, ya.shape[2]), tok), pl.BlockSpec((1, tm, yb.shape[2]), tok),
            pl.BlockSpec((1, tm, yc.shape[2]), tok),
            full(wa), full(wb), full(wc), full(wo), full(norm2), full(wr_hi), full(wr_lo), full(br_col),
        ],
        out_specs=[
            pl.BlockSpec((1, tm, D), tok),
            pl.BlockSpec((tm * SUBLANES, LANES), lambda b, s: (b * (S // tm) + s, 0)),
            pl.BlockSpec((TOP_K, 1, 1, tm), lambda b, s: (0, b * (S // tm) + s, 0, 0)),
            pl.BlockSpec((TOP_K, 1, 1, tm), lambda b, s: (0, b * (S // tm) + s, 0, 0)),
        ],
        out_shape=[
            jax.ShapeDtypeStruct((B, S, D), jnp.float32),
            jax.ShapeDtypeStruct((B * S * SUBLANES, LANES), jnp.float32),
            jax.ShapeDtypeStruct((TOP_K, B * S // tm, 1, tm), jnp.int32),
            jax.ShapeDtypeStruct((TOP_K, B * S // tm, 1, tm), jnp.float32),
        ],
        compiler_params=pltpu.CompilerParams(
            dimension_semantics=("arbitrary", "arbitrary"), vmem_limit_bytes=VMEM_LIMIT),
        name="merge",
    )(x, norm1, wg, bg, ya, yb, yc, wa, wb, wc, wo, norm2, wr_hi, wr_lo, br_col)


def _experts_kernel(nused_ref, bexp_ref, tok_cur_ref, tok_nxt_ref, dst_ref, h2_hbm, roww_ref, wgu_ref, bgu_ref,
                    wd_ref, bd_ref, y4_hbm, xbuf, ybuf, wgu_bf, wd_bf, gsem, ssem):
    b = pl.program_id(0)
    nb = pl.num_programs(0)
    nused = nused_ref[0]
    rows = xbuf.shape[1] // SUBLANES
    F = wd_ref.shape[1]
    slot = b % 2

    def tile(ref, off):
        return ref.at[pl.ds(pl.multiple_of(off, SUBLANES), SUBLANES)]

    def gather(tok_ref, buf_slot):
        for i in range(rows):
            pltpu.make_async_copy(tile(h2_hbm, tok_ref[0, 0, i]), xbuf.at[buf_slot, pl.ds(i * SUBLANES, SUBLANES)],
                                  gsem.at[buf_slot]).start()

    def wait_rows(buf, sem):
        pltpu.make_async_copy(buf, buf, sem).wait()

    @pl.when(b == 0)
    def _():
        gather(tok_cur_ref, 0)
        ybuf[...] = jnp.zeros_like(ybuf)
        n_real = y4_hbm.shape[0] - 2 * rows * SUBLANES
        for s in range(2):
            spare = pltpu.make_async_copy(
                ybuf.at[s], y4_hbm.at[pl.ds(n_real + s * rows * SUBLANES, rows * SUBLANES)], ssem.at[s])
            spare.start()
            spare.wait()

    @pl.when(jnp.logical_and(b >= 2, b - 2 < nused))
    def _():
        wait_rows(ybuf.at[slot], ssem.at[slot])

    changed = jnp.logical_or(b == 0, bexp_ref[b] != bexp_ref[jnp.maximum(b - 1, 0)])

    @pl.when(jnp.logical_and(changed, b < nused))
    def _():
        chunk = 128
        def cast(i, carry):
            r = pl.ds(pl.multiple_of(i * chunk, chunk), chunk)
            wgu_bf[r, :] = wgu_ref[0, r, :].astype(jnp.bfloat16)
            wd_bf[r, :] = wd_ref[0, r, :].astype(jnp.bfloat16)
            return carry
        lax.fori_loop(0, wgu_bf.shape[0] // chunk, cast, 0)

    @pl.when(jnp.logical_or(b == 0, b <= nused))
    def _():
        wait_rows(xbuf.at[slot], gsem.at[slot])

    @pl.when(b < nused)
    def _():
        gather(tok_nxt_ref, 1 - slot)
        x = _load_token_tiles(xbuf.at[slot], rows).astype(jnp.bfloat16)
        gu = _dot(x, wgu_bf[...]) + bgu_ref[0]
        gate = jnp.minimum(gu[:, :F], SWIGLU_LIMIT)
        up = jnp.clip(gu[:, F:], -SWIGLU_LIMIT, SWIGLU_LIMIT)
        act = gate * (1.0 / (1.0 + jnp.exp(-SWIGLU_ALPHA * gate))) * (up + 1.0)
        y = _dot(act.astype(jnp.bfloat16), wd_bf[...]) + bd_ref[0]
        w_row = roww_ref[0]
        eye = lax.broadcasted_iota(jnp.int32, (rows, rows), 0) == lax.broadcasted_iota(jnp.int32, (rows, rows), 1)
        w_hi = w_row.astype(jnp.bfloat16).astype(jnp.float32)
        ones = jnp.ones((rows, LANES), jnp.bfloat16)
        w_col = (_dot(jnp.where(eye, w_hi, 0.0).astype(jnp.bfloat16), ones)
                 + _dot(jnp.where(eye, w_row - w_hi, 0.0).astype(jnp.bfloat16), ones))
        _store_token_tiles(ybuf.at[slot], y * jnp.concatenate([w_col] * (y.shape[1] // LANES), axis=1))
        for i in range(rows):
            pltpu.make_async_copy(ybuf.at[slot, pl.ds(i * SUBLANES, SUBLANES)], tile(y4_hbm, dst_ref[0, 0, i]),
                                  ssem.at[slot]).start()

    @pl.when(b == nb - 1)
    def _():
        @pl.when(jnp.logical_and(b >= 1, b - 1 < nused))
        def _():
            wait_rows(ybuf.at[1 - slot], ssem.at[1 - slot])

        @pl.when(b < nused)
        def _():
            wait_rows(ybuf.at[slot], ssem.at[slot])
            wait_rows(xbuf.at[1 - slot], gsem.at[1 - slot])


def _experts(nused, blk_exp, row_tok, row_dst, h2, row_w, wgu, bgu, wd, bd, n_out_rows):
    D = wgu.shape[1]
    nb = row_tok.shape[0]
    rows = row_tok.shape[2]
    F = wd.shape[1]
    grid_spec = pltpu.PrefetchScalarGridSpec(
        num_scalar_prefetch=2,
        grid=(nb,),
        in_specs=[
            pl.BlockSpec((1, 1, rows), lambda b, nu, be: (b, 0, 0), memory_space=pltpu.SMEM),
            pl.BlockSpec((1, 1, rows), lambda b, nu, be: (jnp.minimum(b + 1, nb - 1), 0, 0), memory_space=pltpu.SMEM),
            pl.BlockSpec((1, 1, rows), lambda b, nu, be: (b, 0, 0), memory_space=pltpu.SMEM),
            pl.BlockSpec(memory_space=pl.ANY),
            pl.BlockSpec((1, 1, rows), lambda b, nu, be: (b, 0, 0)),
            pl.BlockSpec((1, D, 2 * F), lambda b, nu, be: (be[b], 0, 0)),
            pl.BlockSpec((1, 1, 2 * F), lambda b, nu, be: (be[b], 0, 0)),
            pl.BlockSpec((1, F, D), lambda b, nu, be: (be[b], 0, 0)),
            pl.BlockSpec((1, 1, D), lambda b, nu, be: (be[b], 0, 0)),
        ],
        out_specs=pl.BlockSpec(memory_space=pl.ANY),
        scratch_shapes=[
            pltpu.VMEM((2, rows * SUBLANES, LANES), jnp.float32),
            pltpu.VMEM((2, rows * SUBLANES, LANES), jnp.float32),
            pltpu.VMEM((D, 2 * F), jnp.bfloat16),
            pltpu.VMEM((F, D), jnp.bfloat16),
            pltpu.SemaphoreType.DMA((2,)),
            pltpu.SemaphoreType.DMA((2,)),
        ],
    )
    return pl.pallas_call(
        _experts_kernel,
        grid_spec=grid_spec,
        out_shape=jax.ShapeDtypeStruct((n_out_rows * SUBLANES, LANES), jnp.float32),
        compiler_params=pltpu.CompilerParams(
            dimension_semantics=("arbitrary",), vmem_limit_bytes=VMEM_LIMIT),
        name="experts",
    )(nused, blk_exp, row_tok, row_tok, row_dst, h2, row_w, wgu, bgu, wd, bd)


def _combine_kernel(x1_ref, y0_ref, y1_ref, y2_ref, y3_ref, o_ref):
    tm = x1_ref.shape[0]
    for c in range(SUBLANES):
        rows = pl.ds(c, tm, stride=SUBLANES)
        cols = slice(c * LANES, (c + 1) * LANES)
        o_ref[:, cols] = x1_ref[:, cols] + (((y0_ref[rows, :] + y1_ref[rows, :]) + y2_ref[rows, :]) + y3_ref[rows, :])


def _combine(x1, y4):
    T, D = x1.shape
    tm = TOKEN_TILE
    nt = T // tm
    return pl.pallas_call(
        _combine_kernel,
        grid=(nt,),
        in_specs=[pl.BlockSpec((tm, D), lambda i: (i, 0))]
        + [pl.BlockSpec((tm * SUBLANES, LANES), functools.partial(lambda i, k: (k * nt + i, 0), k=k))
           for k in range(TOP_K)],
        out_specs=pl.BlockSpec((tm, D), lambda i: (i, 0)),
        out_shape=jax.ShapeDtypeStruct((T, D), jnp.float32),
        name="combine",
    )(x1, y4, y4, y4, y4)


def _rope_tables(seq):
    inv_freq = 1.0 / (ROPE_THETA ** (jnp.arange(0, HEAD_DIM, 2, dtype=jnp.float32) / HEAD_DIM))
    ang = jnp.arange(seq, dtype=jnp.float32)[:, None] * inv_freq[None, :]
    return jnp.cos(ang), jnp.sin(ang)


def _dispatch(top_idx, gate_w, rows):
    T = top_idx.shape[1]
    A = T * TOP_K
    e_flat = top_idx.reshape(A)
    idx_bits = max(1, (A - 1).bit_length())
    assert idx_bits + (N_EXPERTS - 1).bit_length() < 32
    keys = jnp.sort(e_flat * (1 << idx_bits) + jnp.arange(A, dtype=jnp.int32))
    order = keys & ((1 << idx_bits) - 1)
    experts = jnp.arange(N_EXPERTS, dtype=jnp.int32)
    counts = jnp.sum((e_flat[:, None] == experts[None, :]).astype(jnp.int32), axis=0)
    padded = ((counts + rows - 1) // rows) * rows
    pend = jnp.cumsum(padded)
    pstart = pend - padded
    ustart = jnp.cumsum(counts) - counts
    P = A + N_EXPERTS * rows
    nb = P // rows
    blk_start = jnp.arange(nb, dtype=jnp.int32) * rows
    blk_exp = jnp.minimum(jnp.sum((pend[None, :] <= blk_start[:, None]).astype(jnp.int32), axis=1), N_EXPERTS - 1)
    in_blk = jnp.arange(rows, dtype=jnp.int32)[None, :]
    j = blk_start[:, None] + in_blk - pstart[blk_exp][:, None]
    valid = j < counts[blk_exp][:, None]
    a_flat = jnp.take(order, jnp.clip(ustart[blk_exp][:, None] + j, 0, A - 1).reshape(P))
    w_flat = jnp.take(gate_w.reshape(A), a_flat).reshape(nb, rows)
    a = a_flat.reshape(nb, rows)
    row_tok = jnp.where(valid, a % T, 0)
    row_dst = jnp.where(valid, a, A + (jnp.arange(nb, dtype=jnp.int32)[:, None] % 2) * rows + in_blk)
    row_w = jnp.where(valid, w_flat, 0.0)
    nused = (pend[-1] // rows).astype(jnp.int32).reshape(1)
    return (nused, blk_exp, (row_tok * SUBLANES).reshape(nb, 1, rows), (row_dst * SUBLANES).reshape(nb, 1, rows),
            row_w.reshape(nb, 1, rows), A + 2 * rows)


def kernel(x, mem, norm1, w_in, b_gate, q_norm, k_norm, lambda_q1, lambda_k1, lambda_q2, lambda_k2, diff_subln,
           w_pool, pool_scale, mem_norm, w_mem_kv, mq_norm, mk_norm, w_br_diff, w_br_pool, w_br_mem, w_out, norm2,
           w_router, b_router, w_gate_up, b_gate_up, w_down, b_down):
    B, S, D = x.shape
    depth = norm1.shape[0]
    bf = jnp.bfloat16
    qw = 2 * DIFF_HEADS * HEAD_DIM
    vw = DIFF_HEADS * 2 * HEAD_DIM
    pw = w_pool.shape[1] * w_pool.shape[2]
    mw = MEM_HEADS * HEAD_DIM
    k_off, v_off, pool_off, mq_off, gate_off = qw, 2 * qw, 2 * qw + vw, 2 * qw + vw + pw, 2 * qw + vw + pw + mw
    half = HEAD_DIM // 2

    hpd = (jnp.arange(2)[None, :, None] * (DIFF_HEADS * HEAD_DIM) + jnp.arange(DIFF_HEADS)[:, None, None] * HEAD_DIM
           + jnp.arange(HEAD_DIM)[None, None, :])
    perm = hpd.reshape(-1)
    perm_swapped = (hpd - hpd % HEAD_DIM + (hpd % HEAD_DIM + half) % HEAD_DIM).reshape(-1)
    cos, sin = _rope_tables(S)
    cos_t, sin_t = cos.T, sin.T
    cos_k = jnp.tile(cos, (1, 4))
    sin_k = jnp.tile(jnp.concatenate([-sin, sin], axis=1), (1, 2))
    gidx = jnp.arange(qw) // HEAD_DIM
    gmat = (gidx[:, None] == gidx[None, :]).astype(bf)
    pidx = jnp.arange(pw) // w_pool.shape[2]

    for l in range(depth):
        wi = w_in[l]
        wt = jnp.concatenate([wi[:, perm], wi[:, v_off:pool_off], wi[:, mq_off:gate_off]], axis=1).T.astype(bf)
        wr = jnp.concatenate([wi[:, k_off + perm], wi[:, k_off + perm_swapped], wi[:, pool_off:mq_off]],
                             axis=1).astype(bf)
        wg = wi[:, gate_off:].astype(bf)
        kg = jnp.tile(k_norm[l], 2)[None, :]
        kgs = jnp.tile(jnp.roll(k_norm[l], half), 2)[None, :]
        wpool_bd = jnp.where(pidx[:, None] == pidx[None, :],
                             jnp.tile(w_pool[l].reshape(pw, -1), (1, w_pool.shape[1])), 0.0).astype(bf)
        mk, mv_t = _mem_kv(mem, mem_norm[l][None, :], w_mem_kv[l].T.astype(bf), mk_norm[l][:, None])
        q_t, k, v_t, y_b, y_c = _in_proj(
            x, norm1[l][None, :], wt, wr, gmat, cos_t, sin_t, cos_k, sin_k, q_norm[l][:, None], kg, kgs,
            mq_norm[l][:, None], wpool_bd, pool_scale[l][None, :], mk, mv_t)
        y_a = _diff_attn(q_norm[l], k_norm[l], q_t, k, v_t, lambda_q1[l][None, :], lambda_k1[l][None, :], lambda_q2[l][None, :],
                         lambda_k2[l][None, :], diff_subln[l][:, None])
        wr_t = w_router[l].T
        wr_hi = wr_t.astype(bf)
        wr_lo = (wr_t - wr_hi.astype(jnp.float32)).astype(bf)
        x1, h2, top_idx, gate_w = _merge(
            x, norm1[l][None, :], wg, b_gate[l][None, :], y_a, y_b, y_c, w_br_diff[l].astype(bf),
            w_br_pool[l].astype(bf), w_br_mem[l].astype(bf), w_out[l].astype(bf), norm2[l][None, :], wr_hi, wr_lo,
            b_router[l][:, None])
        T = B * S
        top_idx = top_idx.reshape(TOP_K, T)
        gate_w = gate_w.reshape(TOP_K, T)
        nused, blk_exp, row_tok, row_dst, row_w, P = _dispatch(top_idx, gate_w, EXPERT_ROWS)
        y4 = _experts(nused, blk_exp, row_tok, row_dst, h2, row_w, w_gate_up[l],
                      b_gate_up[l][:, None, :], w_down[l], b_down[l][:, None, :], P)
        x = _combine(x1.reshape(T, D), y4).reshape(B, S, D)
    return x
```
